```python
import functools, math
import jax, jax.numpy as jnp
from jax import lax
import numpy as np

D_MODEL = 1024
BATCH = 4
SEQ = 4096
DEPTH = 4
DEC_BATCH = 32
DEC_SEQ = 8
PAST_LEN = 8192
PAGE_SIZE = 128

N_MIXERS = 4
EXPAND = 2
D_INNER = EXPAND * D_MODEL
A_HEADS = 4
A_HEAD_DIM = D_INNER // A_HEADS
A_CONV = 4
A_CHUNK = 64
B_GROUP = 16
B_GROUPS = D_INNER // B_GROUP
B_STATE = 64
B_CHUNK = 128
B_DT_MIN = 1e-3
B_DT_MAX = 1e-1
C_HEADS = 8
C_QK_DIM = 64
C_V_DIM = D_INNER // C_HEADS
C_PATTERNS = ((128, 1), (512, 4), (2048, 16))
C_MAX_WINDOW = max(w for w, _ in C_PATTERNS)
C_BLOCK = 128
D_WINDOWS = (2, 4, 8, 16)
D_GROUP = D_INNER // len(D_WINDOWS)
D_PREFIX = max(D_WINDOWS) - 1
RMS_EPS = 1e-6
HEAD_NORM_EPS = 1e-6
F32 = jnp.float32

kernel_name = 'hybrid_mlstm_s5_dilated_pool_decoder_step'


def rmsnorm(x, g):
    xf = x.astype(F32)
    y = xf * lax.rsqrt(jnp.mean(xf * xf, axis=-1, keepdims=True) + RMS_EPS)
    return (y * g.astype(F32)).astype(x.dtype)


def cmul(ar, ai, br, bi):
    return ar * br - ai * bi, ar * bi + ai * br


def causal_depthwise_conv(u, prev, w, b):
    width, L = w.shape[0], u.shape[1]
    ext = jnp.concatenate([prev.astype(u.dtype), u], axis=1)
    out = b + sum(ext[:, i:i + L] * w[i] for i in range(width))
    return out, ext[:, L:]


def mlstm_cell(q, k, v, ig, lf, c0, n0, m0):
    bsz, L, H, Dh = q.shape
    lc = A_CHUNK if L % A_CHUNK == 0 else L
    nc = L // lc

    def to_chunks(a):
        return jnp.moveaxis(a.reshape((bsz, nc, lc) + a.shape[2:]), 1, 0)

    causal = jnp.tril(jnp.ones((lc, lc), dtype=bool))

    def step(carry, inp):
        c, n, m = carry
        qc, kc, vc, ic, fc = inp
        F = jnp.cumsum(fc, axis=1)
        dlog = F[:, :, None, :] - F[:, None, :, :] + ic[:, None, :, :]
        dlog = jnp.where(causal[None, :, :, None], dlog, -jnp.inf)
        inter = F + m[:, None, :]
        mt = jnp.maximum(inter, jnp.max(dlog, axis=2))
        w = jnp.exp(dlog - mt[:, :, None, :])
        a = jnp.exp(inter - mt)
        s = jnp.einsum('bthd,bshd->btsh', qc, kc) * w
        num = a[..., None] * jnp.einsum('bthd,bhde->bthe', qc, c) + jnp.einsum('btsh,bshe->bthe', s, vc)
        den = a * jnp.einsum('bthd,bhd->bth', qc, n) + jnp.sum(s, axis=2)
        h = num / jnp.maximum(jnp.abs(den), jnp.exp(-mt))[..., None]
        m_new = mt[:, -1]
        f_tot = F[:, -1]
        decay = jnp.exp(f_tot + m - m_new)
        ws = jnp.exp(f_tot[:, None] - F + ic - m_new[:, None])
        c_new = decay[..., None, None] * c + jnp.einsum('bsh,bshd,bshe->bhde', ws, kc, vc)
        n_new = decay[..., None] * n + jnp.einsum('bsh,bshd->bhd', ws, kc)
        return (c_new, n_new, m_new), h

    xs = (to_chunks(q), to_chunks(k), to_chunks(v), to_chunks(ig), to_chunks(lf))
    (c, n, m), hs = lax.scan(step, (c0, n0, m0), xs)
    h = jnp.moveaxis(hs, 0, 1).reshape(bsz, L, H, Dh)
    return h, c, n, m


def mlstm_mixer(h, conv_prev, c0, n0, m0, w_in, b_gate, conv_w, conv_b, w_q, w_k, w_v, norm_g, skip, w_out):
    bsz, L, _ = h.shape
    E, H, Dh = D_INNER, A_HEADS, A_HEAD_DIM
    xm, z, o_pre, gates = jnp.split(h @ w_in, [E, 2 * E, 3 * E], axis=-1)
    xconv, conv_state = causal_depthwise_conv(xm, conv_prev, conv_w, conv_b)
    xc = jax.nn.silu(xconv)
    xc_h = xc.astype(F32).reshape(bsz, L, H, Dh)
    xm_h = xm.astype(F32).reshape(bsz, L, H, Dh)
    q = jnp.einsum('blhd,hde->blhe', xc_h, w_q.astype(F32))
    k = jnp.einsum('blhd,hde->blhe', xc_h, w_k.astype(F32)) * (Dh ** -0.5)
    v = jnp.einsum('blhd,hde->blhe', xm_h, w_v.astype(F32))
    gates = gates.astype(F32) + b_gate.astype(F32)
    ig, lf = gates[..., :H], jax.nn.log_sigmoid(gates[..., H:])
    hc, c, n, m = mlstm_cell(q, k, v, ig, lf, c0.astype(F32), n0.astype(F32), m0.astype(F32))
    hc = hc * jax.nn.sigmoid(o_pre.astype(F32)).reshape(bsz, L, H, Dh)
    mu = jnp.mean(hc, axis=-1, keepdims=True)
    var = jnp.mean(jnp.square(hc - mu), axis=-1, keepdims=True)
    hn = ((hc - mu) * lax.rsqrt(var + HEAD_NORM_EPS)).reshape(bsz, L, E)
    hn = hn * norm_g.astype(F32) + skip.astype(F32) * xc.astype(F32)
    out = (hn.astype(h.dtype) * jax.nn.silu(z)) @ w_out
    return out, conv_state, c, n, m


def s5_discretise(lam_re, lam_im, log_dt, b_re, b_im):
    lr = jnp.minimum(lam_re, -1e-4)
    li = lam_im
    dt = jnp.exp(log_dt)[:, None]
    mag = jnp.exp(dt * lr)
    a_re, a_im = mag * jnp.cos(dt * li), mag * jnp.sin(dt * li)
    den = lr * lr + li * li
    xr, xi = a_re - 1.0, a_im
    coef_re = (xr * lr + xi * li) / den
    coef_im = (xi * lr - xr * li) / den
    bb_re, bb_im = cmul(coef_re[..., None], coef_im[..., None], b_re, b_im)
    return a_re, a_im, bb_re, bb_im


def s5_scan(ug, bb_re, bb_im, a_re, a_im, c_re, c_im, h0_re, h0_im):
    bsz, L, G, _ = ug.shape
    lc = B_CHUNK if L % B_CHUNK == 0 else L
    nc = L // lc
    xs = jnp.moveaxis(ug.reshape((bsz, nc, lc) + ug.shape[2:]), 1, 0)

    def combine(e1, e2):
        a1r, a1i, b1r, b1i = e1
        a2r, a2i, b2r, b2i = e2
        ar, ai = cmul(a2r, a2i, a1r, a1i)
        br, bi = cmul(a2r, a2i, b1r, b1i)
        return ar, ai, br + b2r, bi + b2i

    def step(carry, u_c):
        hr, hi = carry
        bu_r = jnp.einsum('blgc,gpc->blgp', u_c, bb_re)
        bu_i = jnp.einsum('blgc,gpc->blgp', u_c, bb_im)
        ar = jnp.broadcast_to(a_re, bu_r.shape)
        ai = jnp.broadcast_to(a_im, bu_r.shape)
        pr, pim, sr, si = lax.associative_scan(combine, (ar, ai, bu_r, bu_i), axis=1)
        cr, ci = cmul(pr, pim, hr[:, None], hi[:, None])
        st_r, st_i = sr + cr, si + ci
        y = jnp.einsum('blgp,gcp->blgc', st_r, c_re) - jnp.einsum('blgp,gcp->blgc', st_i, c_im)
        return (st_r[:, -1], st_i[:, -1]), y

    (hr, hi), ys = lax.scan(step, (h0_re, h0_im), xs)
    y = jnp.moveaxis(ys, 0, 1).reshape(bsz, L, G * B_GROUP)
    return y, hr, hi


def s5_mixer(h, h0_re, h0_im, w_in, lam_re, lam_im, log_dt, b_re, b_im, c_re, c_im, d_skip, w_glu, w_out):
    bsz, L, _ = h.shape
    u, z = jnp.split(h @ w_in, 2, axis=-1)
    uf = u.astype(F32)
    a_re, a_im, bb_re, bb_im = s5_discretise(lam_re.astype(F32), lam_im.astype(F32), log_dt.astype(F32),
                                             b_re.astype(F32), b_im.astype(F32))
    ug = uf.reshape(bsz, L, B_GROUPS, B_GROUP)
    y, hr, hi = s5_scan(ug, bb_re, bb_im, a_re, a_im, c_re.astype(F32), c_im.astype(F32),
                        h0_re.astype(F32), h0_im.astype(F32))
    y = y + d_skip.astype(F32) * uf
    g = jax.nn.gelu(y)
    g = g * jax.nn.sigmoid(g @ w_glu.astype(F32))
    out = (g.astype(h.dtype) * jax.nn.silu(z)) @ w_out
    return out, hr, hi


def residue_split(t, d):
    bsz, S = t.shape[:2]
    t = jnp.swapaxes(t.reshape((bsz, S // d, d) + t.shape[2:]), 1, 2)
    return t.reshape((bsz * d, S // d) + t.shape[3:])


def residue_merge(t, bsz, d):
    n = t.shape[1]
    t = jnp.swapaxes(t.reshape((bsz, d, n) + t.shape[2:]), 1, 2)
    return t.reshape((bsz, n * d) + t.shape[3:])


def band_attention(q, k, v, reach):
    N, n, H, dk = q.shape
    dv = v.shape[-1]
    Q = C_BLOCK
    n_pad = -(-n // Q) * Q
    nb = n_pad // Q
    pad = lambda t: jnp.pad(t, ((0, 0), (0, n_pad - n), (0, 0), (0, 0)))
    qb = pad(q).astype(F32).reshape(N, nb, Q, H, dk)
    kb = pad(k).astype(F32).reshape(N, nb, Q, H, dk)
    vb = pad(v).astype(F32).reshape(N, nb, Q, H, dv)
    with_prev = lambda t: jnp.concatenate(
        [jnp.concatenate([jnp.zeros_like(t[:, :1]), t[:, :-1]], axis=1), t], axis=2)
    kk, vv = with_prev(kb), with_prev(vb)
    s = jnp.einsum('nbqhd,nbkhd->nbhqk', qb, kk)
    qpos = jnp.arange(nb)[:, None, None] * Q + jnp.arange(Q)[None, :, None]
    kpos = jnp.arange(nb)[:, None, None] * Q - Q + jnp.arange(2 * Q)[None, None, :]
    dist = qpos - kpos
    mask = (dist >= 0) & (dist <= reach) & (kpos >= 0)
    s = jnp.where(mask[None, :, None], s, -jnp.inf)
    m = jnp.max(s, axis=-1)
    p = jnp.exp(s - m[..., None])
    den = jnp.sum(p, axis=-1)
    num = jnp.einsum('nbhqk,nbkhe->nbqhe', p, vv)
    unblock = lambda t: t.reshape((N, n_pad) + t.shape[3:])[:, :n]
    return unblock(num), unblock(jnp.swapaxes(m, 2, 3)), unblock(jnp.swapaxes(den, 2, 3))


def merge_by_denominator(parts):
    mmax = functools.reduce(jnp.maximum, [m for _, m, _ in parts])
    num = sum(nu * jnp.exp(m - mmax)[..., None] for nu, m, _ in parts)
    den = sum(de * jnp.exp(m - mmax) for _, m, de in parts)
    return num / den[..., None]


def dilated_project(h, w_in):
    bsz, L = h.shape[:2]
    nq = C_HEADS * C_QK_DIM
    q, k, v, z = jnp.split(h @ w_in, [3 * nq, 6 * nq, 6 * nq + C_HEADS * C_V_DIM], axis=-1)
    q = q.reshape(bsz, L, len(C_PATTERNS), C_HEADS, C_QK_DIM) * (C_QK_DIM ** -0.5)
    k = k.reshape(bsz, L, len(C_PATTERNS), C_HEADS, C_QK_DIM)
    v = v.reshape(bsz, L, C_HEADS, C_V_DIM)
    return q, k, v, z


def dilated_prompt(h, w_in, w_out):
    bsz, L = h.shape[:2]
    q, k, v, z = dilated_project(h, w_in)
    parts = []
    for g, (win, dil) in enumerate(C_PATTERNS):
        nu, m, de = band_attention(residue_split(q[:, :, g], dil), residue_split(k[:, :, g], dil),
                                   residue_split(v, dil), win // dil)
        parts.append((residue_merge(nu, bsz, dil), residue_merge(m, bsz, dil), residue_merge(de, bsz, dil)))
    o = merge_by_denominator(parts).reshape(bsz, L, D_INNER)
    out = (o.astype(h.dtype) * jax.nn.silu(z)) @ w_out
    k_bufs = tuple(k[:, L - min(win, L):, g] for g, (win, _) in enumerate(C_PATTERNS))
    v_buf = v[:, L - min(C_MAX_WINDOW, L):]
    return out, k_bufs, v_buf


def dilated_sample(h, k_caches, v_cache, w_in, w_out):
    bsz, L = h.shape[:2]
    q, k, v, z = dilated_project(h, w_in)
    lv = v_cache.shape[1]
    vc = jnp.concatenate([v_cache.astype(v.dtype), v], axis=1)
    i = jnp.arange(L)[:, None]
    parts, new_k = [], []
    for g, (win, dil) in enumerate(C_PATTERNS):
        lk = k_caches[g].shape[1]
        kc = jnp.concatenate([k_caches[g].astype(k.dtype), k[:, :, g]], axis=1)
        back = i - jnp.arange(win // dil + 1)[None, :] * dil
        valid = (lk + back) >= 0
        kg = kc[:, jnp.maximum(lk + back, 0)].astype(F32)
        vg = vc[:, jnp.maximum(lv + back, 0)].astype(F32)
        s = jnp.einsum('bihd,bijhd->bihj', q[:, :, g].astype(F32), kg)
        s = jnp.where(valid[None, :, None, :], s, -jnp.inf)
        m = jnp.max(s, axis=-1)
        p = jnp.exp(s - m[..., None])
        parts.append((jnp.einsum('bihj,bijhe->bihe', p, vg), m, jnp.sum(p, axis=-1)))
        new_k.append(kc[:, L:])
    o = merge_by_denominator(parts).reshape(bsz, L, D_INNER)
    out = (o.astype(h.dtype) * jax.nn.silu(z)) @ w_out
    return out, tuple(new_k), vc[:, L:]


def pool_mixer(h, prefix, start, w_in, w_grp, scale, w_out):
    bsz, L, _ = h.shape
    u, z = jnp.split(h @ w_in, 2, axis=-1)
    uf = u.astype(F32)
    uc = jnp.concatenate([prefix.astype(F32), uf], axis=1)
    cs = jnp.concatenate([jnp.zeros((bsz, 1, D_INNER), F32), jnp.cumsum(uc, axis=1)], axis=1)
    pos = start + jnp.arange(L)
    pooled = []
    for g, w in enumerate(D_WINDOWS):
        lo, hi = g * D_GROUP, (g + 1) * D_GROUP
        tot = cs[:, D_PREFIX + 1:D_PREFIX + 1 + L, lo:hi] - cs[:, D_PREFIX + 1 - w:D_PREFIX + 1 - w + L, lo:hi]
        cnt = jnp.minimum(pos + 1, w).astype(F32)
        pooled.append(tot / cnt[None, :, None])
    mix = jnp.stack(pooled, axis=2) - uf.reshape(bsz, L, len(D_WINDOWS), D_GROUP)
    mix = jnp.einsum('blgc,gcd->blgd', mix, w_grp.astype(F32)).reshape(bsz, L, D_INNER) * scale.astype(F32)
    out = (mix.astype(h.dtype) * jax.nn.silu(z)) @ w_out
    return out, uc[:, L:]


def setup_inputs(seed: int = 0) -> dict:
    key = jax.random.key(seed)
    ks = iter(jax.random.split(key, 64))
    nrm = lambda shape, sc: jax.random.normal(next(ks), shape, F32) * sc
    NA, NB, NC, ND = (len(range(kd, DEPTH, N_MIXERS)) for kd in range(N_MIXERS))
    E, H, Dh = D_INNER, A_HEADS, A_HEAD_DIM
    lk = [min(w, PAST_LEN) for w, _ in C_PATTERNS]
    lv = min(C_MAX_WINDOW, PAST_LEN)
    f_bias = jnp.linspace(A_FGATE_LO, A_FGATE_HI, H)[None] if False else jnp.linspace(3.0, 6.0, H)[None]
    return {
        'x_prompt': nrm((BATCH, SEQ, D_MODEL), 1.0),
        'x_sample': nrm((DEC_BATCH, DEC_SEQ, D_MODEL), 1.0),
        'state_mlstm_c': nrm((NA, DEC_BATCH, H, Dh, Dh), 0.02),
        'state_mlstm_n': nrm((NA, DEC_BATCH, H, Dh), 0.1),
        'state_mlstm_m': nrm((NA, DEC_BATCH, H), 1.0),
        'state_mlstm_conv': nrm((NA, DEC_BATCH, A_CONV - 1, E), 1.0),
        'state_s5_re': nrm((NB, DEC_BATCH, B_GROUPS, B_STATE), 0.1),
        'state_s5_im': nrm((NB, DEC_BATCH, B_GROUPS, B_STATE), 0.1),
        'cache_dil_k1': nrm((NC, DEC_BATCH, lk[0], C_HEADS, C_QK_DIM), 1.0),
        'cache_dil_k2': nrm((NC, DEC_BATCH, lk[1], C_HEADS, C_QK_DIM), 1.0),
        'cache_dil_k3': nrm((NC, DEC_BATCH, lk[2], C_HEADS, C_QK_DIM), 1.0),
        'cache_dil_v': nrm((NC, DEC_BATCH, lv, C_HEADS, C_V_DIM), 1.0),
        'state_pool': nrm((ND, DEC_BATCH, D_PREFIX, E), 1.0),
        'norm_g': 1.0 + nrm((DEPTH, D_MODEL), 0.02),
        'final_norm_g': 1.0 + nrm((D_MODEL,), 0.02),
        'a_w_in': nrm((NA, D_MODEL, 3 * E + 2 * H), D_MODEL ** -0.5),
        'a_b_gate': jnp.concatenate([nrm((NA, H), 0.1), f_bias + nrm((NA, H), 0.1)], axis=-1),
        'a_conv_w': nrm((NA, A_CONV, E), A_CONV ** -0.5),
        'a_conv_b': nrm((NA, E), 0.02),
        'a_w_q': nrm((NA, H, Dh, Dh), Dh ** -0.5),
        'a_w_k': nrm((NA, H, Dh, Dh), Dh ** -0.5),
        'a_w_v': nrm((NA, H, Dh, Dh), Dh ** -0.5),
        'a_norm_g': 1.0 + nrm((NA, E), 0.02),
        'a_skip': 1.0 + nrm((NA, E), 0.02),
        'a_w_out': nrm((NA, E, D_MODEL), E ** -0.5),
        'b_w_in': nrm((NB, D_MODEL, 2 * E), D_MODEL ** -0.5),
        'b_lam_re': -0.5 + nrm((NB, B_GROUPS, B_STATE), 0.01),
        'b_lam_im': jnp.pi * jnp.arange(B_STATE, dtype=F32) + nrm((NB, B_GROUPS, B_STATE), 0.01),
        'b_log_dt': jax.random.uniform(next(ks), (NB, B_GROUPS), F32, math.log(B_DT_MIN), math.log(B_DT_MAX)),
        'b_B_re': nrm((NB, B_GROUPS, B_STATE, B_GROUP), (2 * B_GROUP) ** -0.5),
        'b_B_im': nrm((NB, B_GROUPS, B_STATE, B_GROUP), (2 * B_GROUP) ** -0.5),
        'b_C_re': nrm((NB, B_GROUPS, B_GROUP, B_STATE), B_STATE ** -0.5),
        'b_C_im': nrm((NB, B_GROUPS, B_GROUP, B_STATE), B_STATE ** -0.5),
        'b_d': nrm((NB, E), 1.0),
        'b_w_glu': nrm((NB, E, E), E ** -0.5),
        'b_w_out': nrm((NB, E, D_MODEL), E ** -0.5),
        'c_w_in': nrm((NC, D_MODEL, 6 * C_HEADS * C_QK_DIM + C_HEADS * C_V_DIM + E), D_MODEL ** -0.5),
        'c_w_out': nrm((NC, E, D_MODEL), E ** -0.5),
        'd_w_in': nrm((ND, D_MODEL, 2 * E), D_MODEL ** -0.5),
        'd_w_grp': nrm((ND, len(D_WINDOWS), D_GROUP, D_GROUP), D_GROUP ** -0.5),
        'd_scale': 1.0 + nrm((ND, E), 0.1),
        'd_w_out': nrm((ND, E, D_MODEL), E ** -0.5),
    }


def reference(x_prompt, x_sample, state_mlstm_c, state_mlstm_n, state_mlstm_m, state_mlstm_conv,
              state_s5_re, state_s5_im, cache_dil_k1, cache_dil_k2, cache_dil_k3, cache_dil_v, state_pool,
              norm_g, final_norm_g,
              a_w_in, a_b_gate, a_conv_w, a_conv_b, a_w_q, a_w_k, a_w_v, a_norm_g, a_skip, a_w_out,
              b_w_in, b_lam_re, b_lam_im, b_log_dt, b_B_re, b_B_im, b_C_re, b_C_im, b_d, b_w_glu, b_w_out,
              c_w_in, c_w_out, d_w_in, d_w_grp, d_scale, d_w_out):
    bp, xdt = x_prompt.shape[0], x_prompt.dtype
    names = ('mlstm_c', 'mlstm_n', 'mlstm_m', 'mlstm_conv', 's5_re', 's5_im', 'k1', 'k2', 'k3', 'v', 'pool')
    new_p = {nm: [] for nm in names}
    new_s = {nm: [] for nm in names}
    yp, ys = x_prompt, x_sample
    for layer in range(DEPTH):
        kind, j = layer % N_MIXERS, layer // N_MIXERS
        hp, hs = rmsnorm(yp, norm_g[layer]), rmsnorm(ys, norm_g[layer])
        if kind == 0:
            w = (a_w_in[j], a_b_gate[j], a_conv_w[j], a_conv_b[j], a_w_q[j], a_w_k[j], a_w_v[j],
                 a_norm_g[j], a_skip[j], a_w_out[j])
            op, *sp = mlstm_mixer(hp, jnp.zeros((bp, A_CONV - 1, D_INNER), xdt),
                                  jnp.zeros((bp, A_HEADS, A_HEAD_DIM, A_HEAD_DIM), F32),
                                  jnp.zeros((bp, A_HEADS, A_HEAD_DIM), F32), jnp.zeros((bp, A_HEADS), F32), *w)
            os_, *ss = mlstm_mixer(hs, state_mlstm_conv[j], state_mlstm_c[j], state_mlstm_n[j], state_mlstm_m[j], *w)
            keys = ('mlstm_conv', 'mlstm_c', 'mlstm_n', 'mlstm_m')
        elif kind == 1:
            w = (b_w_in[j], b_lam_re[j], b_lam_im[j], b_log_dt[j], b_B_re[j], b_B_im[j], b_C_re[j], b_C_im[j],
                 b_d[j], b_w_glu[j], b_w_out[j])
            zs = jnp.zeros((bp, B_GROUPS, B_STATE), F32)
            op, *sp = s5_mixer(hp, zs, zs, *w)
            os_, *ss = s5_mixer(hs, state_s5_re[j], state_s5_im[j], *w)
            keys = ('s5_re', 's5_im')
        elif kind == 2:
            op, kp, vp = dilated_prompt(hp, c_w_in[j], c_w_out[j])
            os_, kq, vq = dilated_sample(hs, (cache_dil_k1[j], cache_dil_k2[j], cache_dil_k3[j]), cache_dil_v[j],
                                         c_w_in[j], c_w_out[j])
            sp, ss = (*kp, vp), (*kq, vq)
            keys = ('k1', 'k2', 'k3', 'v')
        else:
            w = (d_w_in[j], d_w_grp[j], d_scale[j], d_w_out[j])
            op, *sp = pool_mixer(hp, jnp.zeros((bp, D_PREFIX, D_INNER), xdt), 0, *w)
            os_, *ss = pool_mixer(hs, state_pool[j], PAST_LEN, *w)
            keys = ('pool',)
        for nm, a, b in zip(keys, sp, ss):
            new_p[nm].append(a)
            new_s[nm].append(b)
        yp, ys = yp + op, ys + os_
    y_prompt = rmsnorm(yp, final_norm_g)
    y_sample = rmsnorm(ys, final_norm_g)
    mlstm_c_p, mlstm_c_s = jnp.stack(new_p['mlstm_c']), jnp.stack(new_s['mlstm_c'])
    mlstm_n_p, mlstm_n_s = jnp.stack(new_p['mlstm_n']), jnp.stack(new_s['mlstm_n'])
    mlstm_m_p, mlstm_m_s = jnp.stack(new_p['mlstm_m']), jnp.stack(new_s['mlstm_m'])
    mlstm_conv_p, mlstm_conv_s = jnp.stack(new_p['mlstm_conv']), jnp.stack(new_s['mlstm_conv'])
    s5_re_p, s5_re_s = jnp.stack(new_p['s5_re']), jnp.stack(new_s['s5_re'])
    s5_im_p, s5_im_s = jnp.stack(new_p['s5_im']), jnp.stack(new_s['s5_im'])
    k1_p, k1_s = jnp.stack(new_p['k1']), jnp.stack(new_s['k1'])
    k2_p, k2_s = jnp.stack(new_p['k2']), jnp.stack(new_s['k2'])
    k3_p, k3_s = jnp.stack(new_p['k3']), jnp.stack(new_s['k3'])
    v_p, v_s = jnp.stack(new_p['v']), jnp.stack(new_s['v'])
    pool_p, pool_s = jnp.stack(new_p['pool']), jnp.stack(new_s['pool'])
    return (y_prompt, y_sample, mlstm_c_p, mlstm_c_s, mlstm_n_p, mlstm_n_s, mlstm_m_p, mlstm_m_s,
            mlstm_conv_p, mlstm_conv_s, s5_re_p, s5_re_s, s5_im_p, s5_im_s, k1_p, k1_s, k2_p, k2_s,
            k3_p, k3_s, v_p, v_s, pool_p, pool_s)
```

```python
import functools
import math

import jax
import jax.numpy as jnp
from jax import lax
from jax.experimental import pallas as pl
from jax.experimental.pallas import tpu as pltpu

F32 = jnp.float32
BF16 = jnp.bfloat16
RMS_EPS = 1e-6
HEAD_NORM_EPS = 1e-6
VMEM_LIMIT = 48 * 1024 * 1024
PAST_LEN = 8192


def _cparams(*sem):
    return pltpu.CompilerParams(dimension_semantics=sem, vmem_limit_bytes=VMEM_LIMIT)


def _sigmoid(x):
    return 1.0 / (1.0 + jnp.exp(-x))


def _norm_matmul_body(x_ref, g_ref, w_ref, o_ref, h_scr):
    @pl.when(pl.program_id(1) == 0)
    def _():
        xf = x_ref[...]
        ms = jnp.mean(xf * xf, axis=-1, keepdims=True)
        h_scr[...] = (xf * lax.rsqrt(ms + RMS_EPS) * g_ref[...]).astype(BF16)

    o_ref[...] = jnp.dot(h_scr[...], w_ref[...], preferred_element_type=F32)


def _norm_matmul(x2d, g, w_bf16, tm, tn):
    m, k = x2d.shape
    n = w_bf16.shape[1]
    tm, tn = min(tm, m), min(tn, n)
    assert m % tm == 0 and n % tn == 0
    return pl.pallas_call(
        _norm_matmul_body,
        grid=(m // tm, n // tn),
        in_specs=[pl.BlockSpec((tm, k), lambda i, j: (i, 0)),
                  pl.BlockSpec((1, k), lambda i, j: (0, 0)),
                  pl.BlockSpec((k, tn), lambda i, j: (0, j))],
        out_specs=pl.BlockSpec((tm, tn), lambda i, j: (i, j)),
        out_shape=jax.ShapeDtypeStruct((m, n), F32),
        scratch_shapes=[pltpu.VMEM((tm, k), BF16)],
        compiler_params=_cparams("arbitrary", "arbitrary"),
        name="norm_matmul",
    )(x2d, g.reshape(1, k), w_bf16)


def _gated_out_body(*refs, final):
    if final:
        x_ref, a_ref, z_ref, w_ref, fg_ref, o_ref = refs
    else:
        x_ref, a_ref, z_ref, w_ref, o_ref = refs
    z = z_ref[...]
    act = (a_ref[...] * (z * _sigmoid(z))).astype(BF16)
    y = x_ref[...] + jnp.dot(act, w_ref[...], preferred_element_type=F32)
    if final:
        ms = jnp.mean(y * y, axis=-1, keepdims=True)
        y = y * lax.rsqrt(ms + RMS_EPS) * fg_ref[...]
    o_ref[...] = y


def _gated_out(x2d, a_src, z_src, w_bf16, tm, final_g=None):
    m, d = x2d.shape
    e = w_bf16.shape[0]
    tm = min(tm, m)
    assert m % tm == 0
    (a_arr, a_blk), (z_arr, z_blk) = a_src, z_src
    in_specs = [pl.BlockSpec((tm, d), lambda i: (i, 0)),
                pl.BlockSpec((tm, e), lambda i: (i, a_blk)),
                pl.BlockSpec((tm, e), lambda i: (i, z_blk)),
                pl.BlockSpec((e, d), lambda i: (0, 0))]
    args = [x2d, a_arr, z_arr, w_bf16]
    if final_g is not None:
        in_specs.append(pl.BlockSpec((1, d), lambda i: (0, 0)))
        args.append(final_g.reshape(1, d))
    return pl.pallas_call(
        functools.partial(_gated_out_body, final=final_g is not None),
        grid=(m // tm,),
        in_specs=in_specs,
        out_specs=pl.BlockSpec((tm, d), lambda i: (i, 0)),
        out_shape=jax.ShapeDtypeStruct((m, d), F32),
        compiler_params=_cparams("arbitrary"),
        name="gated_out",
    )(*args)


A_CHUNK = 64
B_CHUNK = 128
C_BLOCK = 128
C_PATTERNS = ((128, 1), (512, 4), (2048, 16))
D_WINDOWS = (2, 4, 8, 16)


def _cmul(ar, ai, br, bi):
    return ar * br - ai * bi, ar * bi + ai * br


def _conv(u, prev, w, b):
    width, L = w.shape[0], u.shape[1]
    ext = jnp.concatenate([prev.astype(u.dtype), u], axis=1)
    out = b + sum(ext[:, i:i + L] * w[i] for i in range(width))
    return out, ext[:, L:]


def _mlstm_cell(q, k, v, ig, lf, c0, n0, m0):
    bsz, L, H, Dh = q.shape
    lc = A_CHUNK if L % A_CHUNK == 0 else L
    nc = L // lc

    def to_chunks(a):
        return jnp.moveaxis(a.reshape((bsz, nc, lc) + a.shape[2:]), 1, 0)

    causal = jnp.tril(jnp.ones((lc, lc), dtype=bool))

    def step(carry, inp):
        c, n, m = carry
        qc, kc, vc, ic, fc = inp
        F = jnp.cumsum(fc, axis=1)
        dlog = F[:, :, None, :] - F[:, None, :, :] + ic[:, None, :, :]
        dlog = jnp.where(causal[None, :, :, None], dlog, -jnp.inf)
        inter = F + m[:, None, :]
        mt = jnp.maximum(inter, jnp.max(dlog, axis=2))
        w = jnp.exp(dlog - mt[:, :, None, :])
        a = jnp.exp(inter - mt)
        s = jnp.einsum('bthd,bshd->btsh', qc, kc) * w
        num = a[..., None] * jnp.einsum('bthd,bhde->bthe', qc, c) + jnp.einsum('btsh,bshe->bthe', s, vc)
        den = a * jnp.einsum('bthd,bhd->bth', qc, n) + jnp.sum(s, axis=2)
        h = num / jnp.maximum(jnp.abs(den), jnp.exp(-mt))[..., None]
        m_new = mt[:, -1]
        f_tot = F[:, -1]
        decay = jnp.exp(f_tot + m - m_new)
        ws = jnp.exp(f_tot[:, None] - F + ic - m_new[:, None])
        c_new = decay[..., None, None] * c + jnp.einsum('bsh,bshd,bshe->bhde', ws, kc, vc)
        n_new = decay[..., None] * n + jnp.einsum('bsh,bshd->bhd', ws, kc)
        return (c_new, n_new, m_new), h

    xs = (to_chunks(q), to_chunks(k), to_chunks(v), to_chunks(ig), to_chunks(lf))
    (c, n, m), hs = lax.scan(step, (c0, n0, m0), xs)
    h = jnp.moveaxis(hs, 0, 1).reshape(bsz, L, H, Dh)
    return h, c, n, m


def _mlstm_mix(proj, gates, conv_prev, c0, n0, m0, b_gate, conv_w, conv_b, w_q, w_k, w_v, norm_g, skip):
    bsz, L, _ = proj.shape
    H, Dh = w_q.shape[0], w_q.shape[1]
    E = H * Dh
    xm, o_pre = proj[..., :E], proj[..., 2 * E:3 * E]
    xconv, conv_state = _conv(xm, conv_prev, conv_w, conv_b)
    xc = jax.nn.silu(xconv)
    xc_h = xc.reshape(bsz, L, H, Dh)
    xm_h = xm.reshape(bsz, L, H, Dh)
    q = jnp.einsum('blhd,hde->blhe', xc_h, w_q)
    k = jnp.einsum('blhd,hde->blhe', xc_h, w_k) * (Dh ** -0.5)
    v = jnp.einsum('blhd,hde->blhe', xm_h, w_v)
    gates = gates + b_gate
    ig, lf = gates[..., :H], jax.nn.log_sigmoid(gates[..., H:])
    hc, c, n, m = _mlstm_cell(q, k, v, ig, lf, c0, n0, m0)
    hc = hc * jax.nn.sigmoid(o_pre).reshape(bsz, L, H, Dh)
    mu = jnp.mean(hc, axis=-1, keepdims=True)
    var = jnp.mean(jnp.square(hc - mu), axis=-1, keepdims=True)
    hn = ((hc - mu) * lax.rsqrt(var + HEAD_NORM_EPS)).reshape(bsz, L, E)
    hn = hn * norm_g + skip * xc
    return hn, conv_state, c, n, m


def _s5_discretise(lam_re, lam_im, log_dt, b_re, b_im):
    lr = jnp.minimum(lam_re, -1e-4)
    li = lam_im
    dt = jnp.exp(log_dt)[:, None]
    mag = jnp.exp(dt * lr)
    a_re, a_im = mag * jnp.cos(dt * li), mag * jnp.sin(dt * li)
    den = lr * lr + li * li
    xr, xi = a_re - 1.0, a_im
    coef_re = (xr * lr + xi * li) / den
    coef_im = (xi * lr - xr * li) / den
    bb_re, bb_im = _cmul(coef_re[..., None], coef_im[..., None], b_re, b_im)
    return a_re, a_im, bb_re, bb_im


def _s5_scan(ug, bb_re, bb_im, a_re, a_im, c_re, c_im, h0_re, h0_im):
    bsz, L, G, _ = ug.shape
    lc = B_CHUNK if L % B_CHUNK == 0 else L
    nc = L // lc
    xs = jnp.moveaxis(ug.reshape((bsz, nc, lc) + ug.shape[2:]), 1, 0)

    def combine(e1, e2):
        a1r, a1i, b1r, b1i = e1
        a2r, a2i, b2r, b2i = e2
        ar, ai = _cmul(a2r, a2i, a1r, a1i)
        br, bi = _cmul(a2r, a2i, b1r, b1i)
        return ar, ai, br + b2r, bi + b2i

    def step(carry, u_c):
        hr, hi = carry
        bu_r = jnp.einsum('blgc,gpc->blgp', u_c, bb_re)
        bu_i = jnp.einsum('blgc,gpc->blgp', u_c, bb_im)
        ar = jnp.broadcast_to(a_re, bu_r.shape)
        ai = jnp.broadcast_to(a_im, bu_r.shape)
        pr, pim, sr, si = lax.associative_scan(combine, (ar, ai, bu_r, bu_i), axis=1)
        cr, ci = _cmul(pr, pim, hr[:, None], hi[:, None])
        st_r, st_i = sr + cr, si + ci
        y = jnp.einsum('blgp,gcp->blgc', st_r, c_re) - jnp.einsum('blgp,gcp->blgc', st_i, c_im)
        return (st_r[:, -1], st_i[:, -1]), y

    (hr, hi), ys = lax.scan(step, (h0_re, h0_im), xs)
    y = jnp.moveaxis(ys, 0, 1).reshape(bsz, L, -1)
    return y, hr, hi


def _s5_mix(proj, h0_re, h0_im, lam_re, lam_im, log_dt, b_re, b_im, c_re, c_im, d_skip, w_glu):
    bsz, L, _ = proj.shape
    G, P, C = b_re.shape
    E = G * C
    uf = proj[..., :E]
    a_re, a_im, bb_re, bb_im = _s5_discretise(lam_re, lam_im, log_dt, b_re, b_im)
    ug = uf.reshape(bsz, L, G, C)
    y, hr, hi = _s5_scan(ug, bb_re, bb_im, a_re, a_im, c_re, c_im, h0_re, h0_im)
    y = y + d_skip * uf
    g = jax.nn.gelu(y)
    g = g * jax.nn.sigmoid(g @ w_glu)
    return g, hr, hi


def _residue_split(t, d):
    bsz, S = t.shape[:2]
    t = jnp.swapaxes(t.reshape((bsz, S // d, d) + t.shape[2:]), 1, 2)
    return t.reshape((bsz * d, S // d) + t.shape[3:])


def _residue_merge(t, bsz, d):
    n = t.shape[1]
    t = jnp.swapaxes(t.reshape((bsz, d, n) + t.shape[2:]), 1, 2)
    return t.reshape((bsz, n * d) + t.shape[3:])


def _band_attention(q, k, v, reach):
    N, n, H, dk = q.shape
    dv = v.shape[-1]
    Q = C_BLOCK
    n_pad = -(-n // Q) * Q
    nb = n_pad // Q
    pad = lambda t: jnp.pad(t, ((0, 0), (0, n_pad - n), (0, 0), (0, 0)))
    qb = pad(q).reshape(N, nb, Q, H, dk)
    kb = pad(k).reshape(N, nb, Q, H, dk)
    vb = pad(v).reshape(N, nb, Q, H, dv)
    with_prev = lambda t: jnp.concatenate(
        [jnp.concatenate([jnp.zeros_like(t[:, :1]), t[:, :-1]], axis=1), t], axis=2)
    kk, vv = with_prev(kb), with_prev(vb)
    s = jnp.einsum('nbqhd,nbkhd->nbhqk', qb, kk)
    qpos = jnp.arange(nb)[:, None, None] * Q + jnp.arange(Q)[None, :, None]
    kpos = jnp.arange(nb)[:, None, None] * Q - Q + jnp.arange(2 * Q)[None, None, :]
    dist = qpos - kpos
    mask = (dist >= 0) & (dist <= reach) & (kpos >= 0)
    s = jnp.where(mask[None, :, None], s, -jnp.inf)
    m = jnp.max(s, axis=-1)
    p = jnp.exp(s - m[..., None])
    den = jnp.sum(p, axis=-1)
    num = jnp.einsum('nbhqk,nbkhe->nbqhe', p, vv)
    unblock = lambda t: t.reshape((N, n_pad) + t.shape[3:])[:, :n]
    return unblock(num), unblock(jnp.swapaxes(m, 2, 3)), unblock(jnp.swapaxes(den, 2, 3))


def _merge_by_den(parts):
    mmax = functools.reduce(jnp.maximum, [m for _, m, _ in parts])
    num = sum(nu * jnp.exp(m - mmax)[..., None] for nu, m, _ in parts)
    den = sum(de * jnp.exp(m - mmax) for _, m, de in parts)
    return num / den[..., None]


def _dil_split(proj, heads, dk, dv):
    bsz, L, _ = proj.shape
    nq = heads * dk
    npat = len(C_PATTERNS)
    E = heads * dv
    v = proj[..., E:2 * E].reshape(bsz, L, heads, dv)
    q = proj[..., 2 * E:2 * E + npat * nq].reshape(bsz, L, npat, heads, dk) * (dk ** -0.5)
    k = proj[..., 2 * E + npat * nq:2 * E + 2 * npat * nq].reshape(bsz, L, npat, heads, dk)
    return q, k, v


def _dil_prompt(proj, heads, dk, dv):
    bsz, L, _ = proj.shape
    q, k, v = _dil_split(proj, heads, dk, dv)
    parts = []
    for g, (win, dil) in enumerate(C_PATTERNS):
        nu, m, de = _band_attention(_residue_split(q[:, :, g], dil), _residue_split(k[:, :, g], dil),
                                    _residue_split(v, dil), win // dil)
        parts.append((_residue_merge(nu, bsz, dil), _residue_merge(m, bsz, dil), _residue_merge(de, bsz, dil)))
    o = _merge_by_den(parts).reshape(bsz, L, heads * dv)
    k_bufs = tuple(k[:, L - min(win, L):, g] for g, (win, _) in enumerate(C_PATTERNS))
    v_buf = v[:, L - min(C_PATTERNS[-1][0], L):]
    return o, k_bufs, v_buf


def _dil_sample(proj, k_caches, v_cache, heads, dk, dv):
    bsz, L, _ = proj.shape
    q, k, v = _dil_split(proj, heads, dk, dv)
    lv = v_cache.shape[1]
    vc = jnp.concatenate([v_cache, v], axis=1)
    i = jnp.arange(L)[:, None]
    parts, new_k = [], []
    for g, (win, dil) in enumerate(C_PATTERNS):
        lk = k_caches[g].shape[1]
        kc = jnp.concatenate([k_caches[g], k[:, :, g]], axis=1)
        back = i - jnp.arange(win // dil + 1)[None, :] * dil
        valid = (lk + back) >= 0
        kg = kc[:, jnp.maximum(lk + back, 0)]
        vg = vc[:, jnp.maximum(lv + back, 0)]
        s = jnp.einsum('bihd,bijhd->bihj', q[:, :, g], kg)
        s = jnp.where(valid[None, :, None, :], s, -jnp.inf)
        m = jnp.max(s, axis=-1)
        p = jnp.exp(s - m[..., None])
        parts.append((jnp.einsum('bihj,bijhe->bihe', p, vg), m, jnp.sum(p, axis=-1)))
        new_k.append(kc[:, L:])
    o = _merge_by_den(parts).reshape(bsz, L, heads * dv)
    return o, tuple(new_k), vc[:, L:]


def _pool_mix(proj, prefix, start, w_grp, scale):
    bsz, L, _ = proj.shape
    nw, dg = w_grp.shape[0], w_grp.shape[1]
    E = nw * dg
    dp = max(D_WINDOWS) - 1
    uf = proj[..., :E]
    uc = jnp.concatenate([prefix, uf], axis=1)
    cs = jnp.concatenate([jnp.zeros((bsz, 1, E), F32), jnp.cumsum(uc, axis=1)], axis=1)
    pos = start + jnp.arange(L)
    pooled = []
    for g, w in enumerate(D_WINDOWS):
        lo, hi = g * dg, (g + 1) * dg
        tot = cs[:, dp + 1:dp + 1 + L, lo:hi] - cs[:, dp + 1 - w:dp + 1 - w + L, lo:hi]
        cnt = jnp.minimum(pos + 1, w).astype(F32)
        pooled.append(tot / cnt[None, :, None])
    mix = jnp.stack(pooled, axis=2) - uf.reshape(bsz, L, nw, dg)
    mix = jnp.einsum('blgc,gcd->blgd', mix, w_grp).reshape(bsz, L, E) * scale
    return mix, uc[:, L:]


def kernel(x_prompt, x_sample, state_mlstm_c, state_mlstm_n, state_mlstm_m, state_mlstm_conv, state_s5_re, state_s5_im, cache_dil_k1, cache_dil_k2, cache_dil_k3, cache_dil_v, state_pool, norm_g, final_norm_g, a_w_in, a_b_gate, a_conv_w, a_conv_b, a_w_q, a_w_k, a_w_v, a_norm_g, a_skip, a_w_out, b_w_in, b_lam_re, b_lam_im, b_log_dt, b_B_re, b_B_im, b_C_re, b_C_im, b_d, b_w_glu, b_w_out, c_w_in, c_w_out, d_w_in, d_w_grp, d_scale, d_w_out):
    bp, lp, dm = x_prompt.shape
    bs, ls, _ = x_sample.shape
    depth = norm_g.shape[0]
    H, Dh = a_w_q.shape[1], a_w_q.shape[2]
    E = H * Dh
    heads, dk = cache_dil_k1.shape[3], cache_dil_k1.shape[4]
    dv = cache_dil_v.shape[4]
    names = ('mlstm_c', 'mlstm_n', 'mlstm_m', 'mlstm_conv', 's5_re', 's5_im', 'k1', 'k2', 'k3', 'v', 'pool')
    new_p = {nm: [] for nm in names}
    new_s = {nm: [] for nm in names}
    yp = x_prompt.reshape(bp * lp, dm)
    ys = x_sample.reshape(bs * ls, dm)
    TM = 1024

    def proj_both(layer, w):
        wb = w.astype(BF16)
        pp = _norm_matmul(yp, norm_g[layer], wb, TM, 1024)
        ps = _norm_matmul(ys, norm_g[layer], wb, TM, 1024)
        return pp.reshape(bp, lp, -1), ps.reshape(bs, ls, -1)

    for layer in range(depth):
        kind, j = layer % 4, layer // 4
        last = layer == depth - 1
        fg = final_norm_g if last else None
        if kind == 0:
            pp, ps = proj_both(layer, a_w_in[j][:, :3 * E])
            wg = jnp.pad(a_w_in[j][:, 3 * E:], ((0, 0), (0, 128 - 2 * H)))
            gp, gs = proj_both(layer, wg)
            w = (a_b_gate[j], a_conv_w[j], a_conv_b[j], a_w_q[j], a_w_k[j], a_w_v[j], a_norm_g[j], a_skip[j])
            ap, *sp = _mlstm_mix(pp, gp[..., :2 * H], jnp.zeros((bp, a_conv_w.shape[1] - 1, E), F32),
                                 jnp.zeros((bp, H, Dh, Dh), F32), jnp.zeros((bp, H, Dh), F32),
                                 jnp.zeros((bp, H), F32), *w)
            as_, *ss = _mlstm_mix(ps, gs[..., :2 * H], state_mlstm_conv[j], state_mlstm_c[j], state_mlstm_n[j],
                                  state_mlstm_m[j], *w)
            keys = ('mlstm_conv', 'mlstm_c', 'mlstm_n', 'mlstm_m')
            zblk, w_out = 1, a_w_out[j]
        elif kind == 1:
            pp, ps = proj_both(layer, b_w_in[j])
            w = (b_lam_re[j], b_lam_im[j], b_log_dt[j], b_B_re[j], b_B_im[j], b_C_re[j], b_C_im[j], b_d[j], b_w_glu[j])
            zs = jnp.zeros((bp,) + state_s5_re.shape[2:], F32)
            ap, *sp = _s5_mix(pp, zs, zs, *w)
            as_, *ss = _s5_mix(ps, state_s5_re[j], state_s5_im[j], *w)
            keys = ('s5_re', 's5_im')
            zblk, w_out = 1, b_w_out[j]
        elif kind == 2:
            nqk = 6 * heads * dk
            wc = c_w_in[j]
            wc = jnp.concatenate([wc[:, nqk + E:], wc[:, nqk:nqk + E], wc[:, :nqk]], axis=1)
            pp, ps = proj_both(layer, wc)
            ap, kp, vp = _dil_prompt(pp, heads, dk, dv)
            as_, kq, vq = _dil_sample(ps, (cache_dil_k1[j], cache_dil_k2[j], cache_dil_k3[j]), cache_dil_v[j],
                                      heads, dk, dv)
            sp, ss = (*kp, vp), (*kq, vq)
            keys = ('k1', 'k2', 'k3', 'v')
            zblk, w_out = 0, c_w_out[j]
        else:
            pp, ps = proj_both(layer, d_w_in[j])
            ap, *sp = _pool_mix(pp, jnp.zeros((bp, max(D_WINDOWS) - 1, E), F32), 0, d_w_grp[j], d_scale[j])
            as_, *ss = _pool_mix(ps, state_pool[j], PAST_LEN, d_w_grp[j], d_scale[j])
            keys = ('pool',)
            zblk, w_out = 1, d_w_out[j]
        for nm, a, b in zip(keys, sp, ss):
            new_p[nm].append(a)
            new_s[nm].append(b)
        wo = w_out.astype(BF16)
        yp = _gated_out(yp, (ap.reshape(bp * lp, E), 0), (pp.reshape(bp * lp, -1), zblk), wo, 512, fg)
        ys = _gated_out(ys, (as_.reshape(bs * ls, E), 0), (ps.reshape(bs * ls, -1), zblk), wo, 512, fg)
    out = [yp.reshape(bp, lp, dm), ys.reshape(bs, ls, dm)]
    for nm in names:
        out.append(jnp.stack(new_p[nm]))
        out.append(jnp.stack(new_s[nm]))
    return tuple(out)
```

```python
import functools
import math

import jax
import jax.numpy as jnp
from jax import lax
from jax.experimental import pallas as pl
from jax.experimental.pallas import tpu as pltpu

F32 = jnp.float32
BF16 = jnp.bfloat16
RMS_EPS = 1e-6
HEAD_NORM_EPS = 1e-6
LANES = 128
SUBLANES = 8
VMEM_LIMIT = 48 * 1024 * 1024
PAST_LEN = 8192
N_MIXERS = 4
PROJ_TM, PROJ_TN = 1024, 1024
OUT_TM = 512
GLU_TM, GLU_TN = 512, 1024


def _cparams(*sem):
    return pltpu.CompilerParams(dimension_semantics=sem, vmem_limit_bytes=VMEM_LIMIT)


def _sigmoid(x):
    return 1.0 / (1.0 + jnp.exp(-x))


def _dot_nt(a, b):
    return lax.dot_general(a, b, (((1,), (1,)), ((), ())), preferred_element_type=F32)


def _norm_matmul_body(x_ref, g_ref, w_ref, o_ref, h_scr):
    @pl.when(pl.program_id(1) == 0)
    def _():
        xf = x_ref[...]
        ms = jnp.mean(xf * xf, axis=-1, keepdims=True)
        h_scr[...] = (xf * lax.rsqrt(ms + RMS_EPS) * g_ref[...]).astype(BF16)

    o_ref[...] = jnp.dot(h_scr[...], w_ref[...], preferred_element_type=F32)


def _norm_matmul(x2d, g, w_bf16, tm, tn):
    m, k = x2d.shape
    n = w_bf16.shape[1]
    tm, tn = min(tm, m), min(tn, n)
    assert m % tm == 0 and n % tn == 0
    return pl.pallas_call(
        _norm_matmul_body,
        grid=(m // tm, n // tn),
        in_specs=[pl.BlockSpec((tm, k), lambda i, j: (i, 0)),
                  pl.BlockSpec((1, k), lambda i, j: (0, 0)),
                  pl.BlockSpec((k, tn), lambda i, j: (0, j))],
        out_specs=pl.BlockSpec((tm, tn), lambda i, j: (i, j)),
        out_shape=jax.ShapeDtypeStruct((m, n), F32),
        scratch_shapes=[pltpu.VMEM((tm, k), BF16)],
        compiler_params=_cparams("arbitrary", "arbitrary"),
        name="norm_matmul",
    )(x2d, g.reshape(1, k), w_bf16)


def _gated_out_body(*refs, final):
    if final:
        x_ref, a_ref, z_ref, w_ref, fg_ref, o_ref = refs
    else:
        x_ref, a_ref, z_ref, w_ref, o_ref = refs
    z = z_ref[...]
    act = (a_ref[...] * (z * _sigmoid(z))).astype(BF16)
    y = x_ref[...] + jnp.dot(act, w_ref[...], preferred_element_type=F32)
    if final:
        ms = jnp.mean(y * y, axis=-1, keepdims=True)
        y = y * lax.rsqrt(ms + RMS_EPS) * fg_ref[...]
    o_ref[...] = y


def _gated_out(x2d, a_src, z_src, w_bf16, tm, final_g=None):
    m, d = x2d.shape
    e = w_bf16.shape[0]
    tm = min(tm, m)
    assert m % tm == 0
    (a_arr, a_blk), (z_arr, z_blk) = a_src, z_src
    in_specs = [pl.BlockSpec((tm, d), lambda i: (i, 0)),
                pl.BlockSpec((tm, e), lambda i: (i, a_blk)),
                pl.BlockSpec((tm, e), lambda i: (i, z_blk)),
                pl.BlockSpec((e, d), lambda i: (0, 0))]
    args = [x2d, a_arr, z_arr, w_bf16]
    if final_g is not None:
        in_specs.append(pl.BlockSpec((1, d), lambda i: (0, 0)))
        args.append(final_g.reshape(1, d))
    return pl.pallas_call(
        functools.partial(_gated_out_body, final=final_g is not None),
        grid=(m // tm,),
        in_specs=in_specs,
        out_specs=pl.BlockSpec((tm, d), lambda i: (i, 0)),
        out_shape=jax.ShapeDtypeStruct((m, d), F32),
        compiler_params=_cparams("arbitrary"),
        name="gated_out",
    )(*args)


MLSTM_CHUNK = 256
MLSTM_MIN_CHUNK = 128
CONV_HALO = SUBLANES


def _log_sigmoid(x):
    return jnp.minimum(x, 0.0) - jnp.log1p(jnp.exp(-jnp.abs(x)))


def _mlstm_body(xm_ref, op_ref, gt_ref, bg_ref, cw_ref, cb_ref, wq_ref, wk_ref, wv_ref, ng_ref, sk_ref,
                cp_ref, c0_ref, n0_ref, m0_ref,
                hn_ref, cs_ref, c_ref, n_ref, m_ref, ext, *, t_in, t, heads):
    h, c = pl.program_id(0), pl.program_id(2)
    kw = cw_ref.shape[0]
    dh = xm_ref.shape[1]
    lo = CONV_HALO - (kw - 1)

    @pl.when(c == 0)
    def _():
        c_ref[...] = c0_ref[...]
        n_ref[...] = n0_ref[...]
        m_ref[...] = m0_ref[...]
        ext[lo:CONV_HALO, :] = cp_ref[0]
        if t_in < t:
            ext[CONV_HALO + t_in:CONV_HALO + t, :] = jnp.zeros((t - t_in, dh), F32)

    @pl.when(c > 0)
    def _():
        ext[lo:CONV_HALO, :] = ext[lo + t_in:CONV_HALO + t_in, :]

    ext[CONV_HALO:CONV_HALO + t_in, :] = xm_ref[...]
    cs_ref[0] = ext[lo + t_in:CONV_HALO + t_in, :]

    xm = ext[CONV_HALO:CONV_HALO + t, :]
    xconv = cb_ref[...] + ext[lo:lo + t, :] * cw_ref[0:1, :]
    for i in range(1, kw):
        xconv = xconv + ext[lo + i:lo + i + t, :] * cw_ref[i:i + 1, :]
    xc = xconv * _sigmoid(xconv)
    xcb = xc.astype(BF16)
    q = jnp.dot(xcb, wq_ref[0], preferred_element_type=F32)
    k = jnp.dot(xcb, wk_ref[0], preferred_element_type=F32) * (dh ** -0.5)
    v = jnp.dot(xm.astype(BF16), wv_ref[0], preferred_element_type=F32)

    gt = gt_ref[...] + bg_ref[...]
    row = lax.broadcasted_iota(jnp.int32, (t, 1), 0)
    if t_in < t:
        gt = jnp.concatenate([gt, jnp.zeros((t - t_in, gt.shape[1]), F32)], axis=0)
        valid = row < t_in
        ig_all = jnp.where(valid, gt, -jnp.inf)
        f_all = jnp.where(valid, _log_sigmoid(gt), 0.0)
    else:
        ig_all = gt
        f_all = _log_sigmoid(gt)
    sh = 1
    while sh < t:
        f_all = f_all + jnp.where(row >= sh, pltpu.roll(f_all, sh, 0), 0.0)
        sh *= 2
    lane = lax.broadcasted_iota(jnp.int32, (1, gt.shape[1]), 1)
    sub = lax.broadcasted_iota(jnp.int32, (gt.shape[1], 1), 0)
    col_of = lambda x, idx: jnp.sum(jnp.where(lane == idx, x, 0.0), axis=1, keepdims=True)
    row_of = lambda xt, idx: jnp.sum(jnp.where(sub == idx, xt, 0.0), axis=0, keepdims=True)
    f_col, ig_col = col_of(f_all, heads + h), col_of(ig_all, h)
    f_row, ig_row = row_of(f_all.T, heads + h), row_of(ig_all.T, h)

    m = m_ref[0]
    colidx = lax.broadcasted_iota(jnp.int32, (1, t), 1)
    dlog = jnp.where(row >= colidx, f_col - f_row + ig_row, -jnp.inf)
    inter = f_col + m
    mt = jnp.maximum(inter, jnp.max(dlog, axis=1, keepdims=True))
    w = jnp.exp(dlog - mt)
    a = jnp.exp(inter - mt)
    qb, kb, vb = q.astype(BF16), k.astype(BF16), v.astype(BF16)
    s = _dot_nt(qb, kb) * w
    cmat = c_ref[0]
    num = a * jnp.dot(qb, cmat.astype(BF16), preferred_element_type=F32) \
        + jnp.dot(s.astype(BF16), vb, preferred_element_type=F32)
    nvec = n_ref[0]
    den = a * jnp.sum(q * nvec, axis=1, keepdims=True) + jnp.sum(s, axis=1, keepdims=True)
    hloc = num / jnp.maximum(jnp.abs(den), jnp.exp(-mt))
    m_new = mt[t - 1:t, :]
    f_tot = f_col[t - 1:t, :]
    decay = jnp.exp(f_tot + m - m_new)
    ws = jnp.exp(f_tot - f_col + ig_col - m_new)
    kws = k * ws
    c_ref[0] = decay * cmat + lax.dot_general(kws.astype(BF16), vb, (((0,), (0,)), ((), ())),
                                              preferred_element_type=F32)
    n_ref[0] = decay * nvec + jnp.sum(kws, axis=0, keepdims=True)
    m_ref[0] = m_new

    o = hloc[:t_in, :] * _sigmoid(op_ref[...])
    mu = jnp.mean(o, axis=1, keepdims=True)
    var = jnp.mean(jnp.square(o - mu), axis=1, keepdims=True)
    hn = (o - mu) * lax.rsqrt(var + HEAD_NORM_EPS)
    hn_ref[...] = hn * ng_ref[...] + sk_ref[...] * xc[:t_in, :]


def _mlstm_pallas(proj2d, gates2d, bsz, seq, t_in, t, conv_prev, c0, n0, m0, b_gate, conv_w, conv_b, w_q, w_k, w_v,
                  norm_g, skip):
    heads, dh, _ = w_q.shape
    e = heads * dh
    kw = conv_w.shape[0]
    gl = gates2d.shape[1]
    assert seq % t_in == 0 and t_in <= t and t_in % SUBLANES == 0 and kw - 1 <= min(CONV_HALO, t_in)
    nc = seq // t_in
    rows = lambda blk: pl.BlockSpec((t_in, dh), lambda h, b, c: (b * nc + c, blk(h)))
    per_head_vec = lambda r: pl.BlockSpec((r, dh), lambda h, b, c: (0, h))
    wspec = pl.BlockSpec((1, dh, dh), lambda h, b, c: (h, 0, 0))
    st = lambda shp: pl.BlockSpec((1,) + shp, lambda h, b, c: (b * heads + h, 0, 0))
    cpspec = pl.BlockSpec((1, kw - 1, dh), lambda h, b, c: (b, 0, h))
    hn, cs, c_new, n_new, m_new = pl.pallas_call(
        functools.partial(_mlstm_body, t_in=t_in, t=t, heads=heads),
        grid=(heads, bsz, nc),
        in_specs=[rows(lambda h: h), rows(lambda h: 2 * heads + h),
                  pl.BlockSpec((t_in, gl), lambda h, b, c: (b * nc + c, 0)),
                  pl.BlockSpec((1, gl), lambda h, b, c: (0, 0)),
                  per_head_vec(kw), per_head_vec(1), wspec, wspec, wspec, per_head_vec(1), per_head_vec(1),
                  cpspec, st((dh, dh)), st((1, dh)), st((1, 1))],
        out_specs=[rows(lambda h: h), cpspec, st((dh, dh)), st((1, dh)), st((1, 1))],
        out_shape=[jax.ShapeDtypeStruct((bsz * seq, e), F32), jax.ShapeDtypeStruct((bsz, kw - 1, e), F32),
                   jax.ShapeDtypeStruct((bsz * heads, dh, dh), F32), jax.ShapeDtypeStruct((bsz * heads, 1, dh), F32),
                   jax.ShapeDtypeStruct((bsz * heads, 1, 1), F32)],
        scratch_shapes=[pltpu.VMEM((CONV_HALO + t, dh), F32)],
        compiler_params=_cparams("arbitrary", "arbitrary", "arbitrary"),
        name="mlstm",
    )(proj2d, proj2d, gates2d, jnp.pad(b_gate, (0, gl - b_gate.shape[0])).reshape(1, gl),
      conv_w, conv_b.reshape(1, e), w_q.astype(BF16), w_k.astype(BF16), w_v.astype(BF16),
      norm_g.reshape(1, e), skip.reshape(1, e), conv_prev,
      c0.reshape(bsz * heads, dh, dh), n0.reshape(bsz * heads, 1, dh), m0.reshape(bsz * heads, 1, 1))
    return (hn, cs, c_new.reshape(bsz, heads, dh, dh), n_new.reshape(bsz, heads, dh), m_new.reshape(bsz, heads))


def _s5_disc_body(lr_ref, li_ref, ldt_ref, br_ref, bi_ref, ar_ref, ai_ref, bbr_ref, bbi_ref):
    lr = jnp.minimum(lr_ref[...], -1e-4)
    li = li_ref[...]
    dt = jnp.exp(ldt_ref[...])
    mag = jnp.exp(dt * lr)
    a_re, a_im = mag * jnp.cos(dt * li), mag * jnp.sin(dt * li)
    den = lr * lr + li * li
    xr, xi = a_re - 1.0, a_im
    cr = (xr * lr + xi * li) / den
    ci = (xi * lr - xr * li) / den
    ar_ref[...] = a_re
    ai_ref[...] = a_im
    b_r, b_i = br_ref[...], bi_ref[...]
    bbr_ref[...] = cr * b_r - ci * b_i
    bbi_ref[...] = cr * b_i + ci * b_r


def _s5_discretise_pallas(lam_re, lam_im, log_dt, b_re, b_im):
    g, p, c = b_re.shape
    vm = pl.BlockSpec(memory_space=pltpu.VMEM)
    a_re, a_im, bb_re, bb_im = pl.pallas_call(
        _s5_disc_body,
        in_specs=[vm] * 5,
        out_specs=[vm] * 4,
        out_shape=[jax.ShapeDtypeStruct((g, 1, p), F32)] * 2 + [jax.ShapeDtypeStruct((g, c, p), F32)] * 2,
        name="s5_discretise",
    )(lam_re.reshape(g, 1, p), lam_im.reshape(g, 1, p), log_dt.reshape(g, 1, 1),
      jnp.swapaxes(b_re, 1, 2), jnp.swapaxes(b_im, 1, 2))
    return a_re.reshape(g, p), a_im.reshape(g, p), bb_re, bb_im


S5_STRIP_GROUPS = 8
S5_CHUNK = 256


def _gelu_tanh(x):
    return x * (0.5 * (1.0 + jnp.tanh(math.sqrt(2.0 / math.pi) * (x + 0.044715 * (x * x * x)))))


def _s5_body(u_ref, wbr_ref, wbi_ref, wcr_ref, wci_ref, ar_ref, ai_ref, d_ref, h0r_ref, h0i_ref,
             g_ref, hr_ref, hi_ref, *, nstage):
    c, s = pl.program_id(1), pl.program_id(2)

    @pl.when(c == 0)
    def _():
        hr_ref[s] = h0r_ref[s]
        hi_ref[s] = h0i_ref[s]

    u = u_ref[...]
    t = u.shape[0]
    ub = u.astype(BF16)
    xr = jnp.dot(ub, wbr_ref[s], preferred_element_type=F32)
    xi = jnp.dot(ub, wbi_ref[s], preferred_element_type=F32)
    ar, ai = ar_ref[s], ai_ref[s]
    hr, hi = hr_ref[s], hi_ref[s]
    row = lax.broadcasted_iota(jnp.int32, (t, 1), 0)
    first = row == 0
    xr = xr + jnp.where(first, ar * hr - ai * hi, 0.0)
    xi = xi + jnp.where(first, ar * hi + ai * hr, 0.0)
    pr, pi = ar, ai
    for k in range(nstage):
        sh = 1 << k
        keep = row >= sh
        tr = jnp.where(keep, pltpu.roll(xr, sh, 0), 0.0)
        ti = jnp.where(keep, pltpu.roll(xi, sh, 0), 0.0)
        xr, xi = xr + (pr * tr - pi * ti), xi + (pr * ti + pi * tr)
        pr, pi = pr * pr - pi * pi, 2.0 * (pr * pi)
    hr_ref[s] = xr[t - 1:t, :]
    hi_ref[s] = xi[t - 1:t, :]
    y = (jnp.dot(xr.astype(BF16), wcr_ref[s], preferred_element_type=F32)
         - jnp.dot(xi.astype(BF16), wci_ref[s], preferred_element_type=F32))
    g_ref[...] = _gelu_tanh(y + d_ref[...] * u)


def _s5_pallas(proj2d, bsz, seq, chunk, h0_re, h0_im, a_re, a_im, bb_re, bb_im, c_re, c_im, d_skip):
    g, cch, p = bb_re.shape
    e = g * cch
    sg = S5_STRIP_GROUPS
    ns = g // sg
    lu, ls = sg * cch, sg * p
    assert seq % chunk == 0 and (chunk & (chunk - 1)) == 0 and g % sg == 0 and lu == LANES
    nc = seq // chunk
    eye = jnp.eye(sg, dtype=F32)
    wb = lambda bb: jnp.einsum('sgcp,gh->sgchp', bb.reshape(ns, sg, cch, p), eye).reshape(ns, lu, ls).astype(BF16)
    wc = lambda cc: jnp.einsum('sgcp,gh->sgphc', cc.reshape(ns, sg, cch, p), eye).reshape(ns, ls, lu).astype(BF16)
    const3 = lambda shp: pl.BlockSpec(shp, lambda b, c, s: (0, 0, 0))
    h0spec = pl.BlockSpec((ns, 1, ls), lambda b, c, s: (b, 0, 0))
    gout, hr, hi = pl.pallas_call(
        functools.partial(_s5_body, nstage=chunk.bit_length() - 1),
        grid=(bsz, nc, ns),
        in_specs=[pl.BlockSpec((chunk, lu), lambda b, c, s: (b * nc + c, s)),
                  const3((ns, lu, ls)), const3((ns, lu, ls)), const3((ns, ls, lu)), const3((ns, ls, lu)),
                  const3((ns, 1, ls)), const3((ns, 1, ls)),
                  pl.BlockSpec((1, lu), lambda b, c, s: (0, s)),
                  h0spec, h0spec],
        out_specs=[pl.BlockSpec((chunk, lu), lambda b, c, s: (b * nc + c, s)), h0spec, h0spec],
        out_shape=[jax.ShapeDtypeStruct((bsz * seq, e), F32),
                   jax.ShapeDtypeStruct((bsz * ns, 1, ls), F32), jax.ShapeDtypeStruct((bsz * ns, 1, ls), F32)],
        compiler_params=_cparams("arbitrary", "arbitrary", "arbitrary"),
        name="s5_scan",
    )(proj2d, wb(bb_re), wb(bb_im), wc(c_re), wc(c_im),
      a_re.reshape(ns, 1, ls), a_im.reshape(ns, 1, ls), d_skip.reshape(1, e),
      h0_re.reshape(bsz * ns, 1, ls), h0_im.reshape(bsz * ns, 1, ls))
    return gout, hr.reshape(bsz, g, p), hi.reshape(bsz, g, p)


def _glu_body(g_ref, gc_ref, w_ref, o_ref, gb_scr):
    @pl.when(pl.program_id(1) == 0)
    def _():
        gb_scr[...] = g_ref[...].astype(BF16)

    o_ref[...] = gc_ref[...] * _sigmoid(jnp.dot(gb_scr[...], w_ref[...], preferred_element_type=F32))


def _glu(g2d, w_bf16, tm, tn):
    m, e = g2d.shape
    tm, tn = min(tm, m), min(tn, e)
    assert m % tm == 0 and e % tn == 0
    return pl.pallas_call(
        _glu_body,
        grid=(m // tm, e // tn),
        in_specs=[pl.BlockSpec((tm, e), lambda i, j: (i, 0)),
                  pl.BlockSpec((tm, tn), lambda i, j: (i, j)),
                  pl.BlockSpec((e, tn), lambda i, j: (0, j))],
        out_specs=pl.BlockSpec((tm, tn), lambda i, j: (i, j)),
        out_shape=jax.ShapeDtypeStruct((m, e), F32),
        scratch_shapes=[pltpu.VMEM((tm, e), BF16)],
        compiler_params=_cparams("arbitrary", "arbitrary"),
        name="glu",
    )(g2d, g2d, w_bf16)


POOL_WINDOWS = (2, 4, 8, 16)
POOL_HALO = 2 * SUBLANES
POOL_CHUNK = 512


def _pool_body(u_ref, pre_ref, w_ref, sc_ref, o_ref, st_ref, ext, *, windows, start):
    c = pl.program_id(1)
    t = u_ref.shape[0]
    dp = max(windows) - 1
    dg = u_ref.shape[1] // len(windows)

    @pl.when(c == 0)
    def _():
        ext[POOL_HALO - dp:POOL_HALO, :] = pre_ref[0]

    @pl.when(c > 0)
    def _():
        ext[POOL_HALO - dp:POOL_HALO, :] = ext[POOL_HALO + t - dp:POOL_HALO + t, :]

    ext[POOL_HALO:POOL_HALO + t, :] = u_ref[...]
    st_ref[0] = ext[POOL_HALO + t - dp:POOL_HALO + t, :]
    pos = start + c * t + lax.broadcasted_iota(jnp.int32, (t, 1), 0)
    for g, w in enumerate(windows):
        lo = g * dg
        cur = ext[POOL_HALO:POOL_HALO + t, lo:lo + dg]
        tot = cur
        for j in range(1, w):
            tot = tot + ext[POOL_HALO - j:POOL_HALO - j + t, lo:lo + dg]
        cnt = jnp.minimum(pos + 1, w).astype(F32)
        mix = (tot / cnt - cur).astype(BF16)
        o_ref[:, lo:lo + dg] = jnp.dot(mix, w_ref[g], preferred_element_type=F32) * sc_ref[:, lo:lo + dg]


def _pool_pallas(proj2d, bsz, seq, chunk, prefix, start, w_grp, scale, windows):
    nw, dg, _ = w_grp.shape
    e = nw * dg
    dp = max(windows) - 1
    assert seq % chunk == 0 and prefix.shape == (bsz, dp, e) and dp < POOL_HALO and nw == len(windows)
    nc = seq // chunk
    return pl.pallas_call(
        functools.partial(_pool_body, windows=windows, start=start),
        grid=(bsz, nc),
        in_specs=[pl.BlockSpec((chunk, e), lambda b, c: (b * nc + c, 0)),
                  pl.BlockSpec((1, dp, e), lambda b, c: (b, 0, 0)),
                  pl.BlockSpec((nw, dg, dg), lambda b, c: (0, 0, 0)),
                  pl.BlockSpec((1, e), lambda b, c: (0, 0))],
        out_specs=[pl.BlockSpec((chunk, e), lambda b, c: (b * nc + c, 0)),
                   pl.BlockSpec((1, dp, e), lambda b, c: (b, 0, 0))],
        out_shape=[jax.ShapeDtypeStruct((bsz * seq, e), F32), jax.ShapeDtypeStruct((bsz, dp, e), F32)],
        scratch_shapes=[pltpu.VMEM((POOL_HALO + chunk, e), F32)],
        compiler_params=_cparams("arbitrary", "arbitrary"),
        name="pool",
    )(proj2d, prefix, w_grp.astype(BF16), scale.reshape(1, e))


DIL_PATTERNS = ((128, 1), (512, 4), (2048, 16))
DIL_BLOCK = 128
DIL_TQ = 2048


def _dil_prompt_body(*refs, dils, dk, nvp):
    q0_ref, q1_ref, q2_ref, k0_ref, k1_ref, k2_ref = refs[:6]
    v_refs = refs[6:6 + nvp]
    o_ref = refs[6 + nvp]
    m_scr, l_scr = refs[7 + nvp:9 + nvp]
    acc = refs[9 + nvp:]
    h, i = pl.program_id(1), pl.program_id(2)
    tq = o_ref.shape[0]
    qb = DIL_BLOCK
    nsub = tq // qb
    lane = lax.broadcasted_iota(jnp.int32, (1, LANES), 1)
    mine = (lane // dk) == (h % (LANES // dk))
    rowi = lax.broadcasted_iota(jnp.int32, (qb, 1), 0)
    coli = lax.broadcasted_iota(jnp.int32, (1, qb), 1)
    scale = dk ** -0.5
    for g, d in enumerate(dils):
        q_ref, k_ref = (q0_ref, q1_ref, q2_ref)[g], (k0_ref, k1_ref, k2_ref)[g]
        ld = d.bit_length() - 1

        def body(idx, carry, q_ref=q_ref, k_ref=k_ref, d=d, ld=ld, g=g):
            r, bl = idx & (d - 1), idx >> ld
            lstart = r + (d * qb) * bl
            gstart = i * tq + lstart
            has_prev = gstart >= d * qb
            pstart = jnp.where(has_prev, gstart - d * qb, gstart)
            rows = lambda s: pl.ds(s, qb, stride=d) if d > 1 else pl.ds(s, qb)
            qm = (jnp.where(mine, q_ref[rows(lstart), :], 0.0) * scale).astype(BF16)
            sc = _dot_nt(qm, k_ref[rows(gstart), :].astype(BF16))
            sp = _dot_nt(qm, k_ref[rows(pstart), :].astype(BF16))
            sc = jnp.where(coli <= rowi, sc, -jnp.inf)
            sp = jnp.where((coli >= rowi) & has_prev, sp, -jnp.inf)
            mb = jnp.maximum(jnp.max(sc, axis=1, keepdims=True), jnp.max(sp, axis=1, keepdims=True))
            pc, pp = jnp.exp(sc - mb), jnp.exp(sp - mb)
            lb = jnp.sum(pc, axis=1, keepdims=True) + jnp.sum(pp, axis=1, keepdims=True)
            vrows = lambda s: jnp.concatenate([v[rows(s), :] for v in v_refs], axis=1).astype(BF16)
            nb = (jnp.dot(pc.astype(BF16), vrows(gstart), preferred_element_type=F32)
                  + jnp.dot(pp.astype(BF16), vrows(pstart), preferred_element_type=F32))
            if g > 0:
                mo, lo = m_scr[rows(lstart), :][:, 0:1], l_scr[rows(lstart), :][:, 0:1]
                mn = jnp.maximum(mo, mb)
                eo, eb = jnp.exp(mo - mn), jnp.exp(mb - mn)
                nb = jnp.concatenate([a[rows(lstart), :] for a in acc], axis=1) * eo + nb * eb
                lb = lo * eo + lb * eb
                mb = mn
            for p, a in enumerate(acc):
                a[rows(lstart), :] = nb[:, p * LANES:(p + 1) * LANES]
            m_scr[rows(lstart), :] = jnp.broadcast_to(mb, (qb, LANES))
            l_scr[rows(lstart), :] = jnp.broadcast_to(lb, (qb, LANES))
            return carry

        lax.fori_loop(0, nsub, body, 0)
    den = l_scr[:, 0:1]
    for p, a in enumerate(acc):
        o_ref[:, p * LANES:(p + 1) * LANES] = a[...] / den


def _dil_prompt_pallas(proj2d, bsz, seq, heads, dk, dv, col_q, col_k, col_v):
    dils = tuple(d for _, d in DIL_PATTERNS)
    assert all(w == d * DIL_BLOCK for w, d in DIL_PATTERNS)
    tq = min(DIL_TQ, seq)
    assert seq % tq == 0 and tq % (DIL_BLOCK * max(dils)) == 0 and LANES % dk == 0 and dv % LANES == 0
    nq = seq // tq
    hpb = LANES // dk
    nqk = heads * dk
    nvp = dv // LANES
    qspec = lambda g: pl.BlockSpec((tq, LANES), lambda b, h, i: (b * nq + i, (col_q + g * nqk) // LANES + h // hpb))
    kspec = lambda g: pl.BlockSpec((seq, LANES), lambda b, h, i: (b, (col_k + g * nqk) // LANES + h // hpb))
    vspec = lambda p: pl.BlockSpec((seq, LANES), lambda b, h, i: (b, col_v // LANES + h * nvp + p))
    return pl.pallas_call(
        functools.partial(_dil_prompt_body, dils=dils, dk=dk, nvp=nvp),
        grid=(bsz, heads, nq),
        in_specs=[qspec(0), qspec(1), qspec(2), kspec(0), kspec(1), kspec(2)] + [vspec(p) for p in range(nvp)],
        out_specs=pl.BlockSpec((tq, dv), lambda b, h, i: (b * nq + i, h)),
        out_shape=jax.ShapeDtypeStruct((bsz * seq, heads * dv), F32),
        scratch_shapes=[pltpu.VMEM((tq, LANES), F32)] * (2 + nvp),
        compiler_params=_cparams("arbitrary", "arbitrary", "arbitrary"),
        name="dilated_prompt",
    )(*([proj2d] * (6 + nvp)))


def _dil_sample_body(q0_ref, q1_ref, q2_ref, n0_ref, n1_ref, n2_ref, vn_ref, c0_ref, c1_ref, c2_ref, vc_ref,
                     o_ref, oc0_ref, oc1_ref, oc2_ref, ovc_ref, *, wins, dils, dk):
    ls = q0_ref.shape[0]
    dvb = vn_ref.shape[1]
    hpb = LANES // dk
    lv = vc_ref.shape[1]
    lane = lax.broadcasted_iota(jnp.int32, (1, LANES), 1)
    qi = lax.broadcasted_iota(jnp.int32, (hpb * ls, 1), 0) % ls
    scale = dk ** -0.5
    vnew = vn_ref[...]
    parts = []
    for g, (win, d) in enumerate(zip(wins, dils)):
        q_ref, n_ref, c_ref = (q0_ref, q1_ref, q2_ref)[g], (n0_ref, n1_ref, n2_ref)[g], (c0_ref, c1_ref, c2_ref)[g]
        lk = c_ref.shape[1]
        qt = q_ref[...] * scale
        qbd = jnp.concatenate([jnp.where((lane // dk) == hh, qt, 0.0) for hh in range(hpb)], axis=0).astype(BF16)
        knew = n_ref[...]
        s1 = _dot_nt(qbd, c_ref[0].astype(BF16))
        s2 = _dot_nt(qbd, knew.astype(BF16))
        back1 = lk + qi - lax.broadcasted_iota(jnp.int32, (1, lk), 1)
        back2 = qi - lax.broadcasted_iota(jnp.int32, (1, ls), 1)
        s1 = jnp.where((back1 <= win) & ((back1 & (d - 1)) == 0), s1, -jnp.inf)
        s2 = jnp.where((back2 >= 0) & ((back2 & (d - 1)) == 0), s2, -jnp.inf)
        m = jnp.maximum(jnp.max(s1, axis=1, keepdims=True), jnp.max(s2, axis=1, keepdims=True))
        p1, p2 = jnp.exp(s1 - m), jnp.exp(s2 - m)
        den = jnp.sum(p1, axis=1, keepdims=True) + jnp.sum(p2, axis=1, keepdims=True)
        num = (jnp.dot(p1.astype(BF16), vc_ref[0, lv - lk:lv, :].astype(BF16), preferred_element_type=F32)
               + jnp.dot(p2.astype(BF16), vnew.astype(BF16), preferred_element_type=F32))
        parts.append((num, m, den))
        oc_ref = (oc0_ref, oc1_ref, oc2_ref)[g]
        oc_ref[0, 0:lk - ls, :] = c_ref[0, ls:lk, :]
        oc_ref[0, lk - ls:lk, :] = knew
    mmax = functools.reduce(jnp.maximum, [m for _, m, _ in parts])
    num = sum(nu * jnp.exp(m - mmax) for nu, m, _ in parts)
    den = sum(de * jnp.exp(m - mmax) for _, m, de in parts)
    o = num / den
    dvh = dvb // hpb
    o_ref[...] = jnp.concatenate([o[hh * ls:(hh + 1) * ls, hh * dvh:(hh + 1) * dvh] for hh in range(hpb)], axis=1)
    ovc_ref[0, 0:lv - ls, :] = vc_ref[0, ls:lv, :]
    ovc_ref[0, lv - ls:lv, :] = vnew


def _dil_sample_pallas(proj2d, bsz, ls, k_caches, v_cache, col_q, col_k, col_v):
    heads, dk = k_caches[0].shape[2], k_caches[0].shape[3]
    dv = v_cache.shape[3]
    wins, dils = tuple(w for w, _ in DIL_PATTERNS), tuple(d for _, d in DIL_PATTERNS)
    hpb = LANES // dk
    nqk = heads * dk
    lks = tuple(kc.shape[1] for kc in k_caches)
    lv = v_cache.shape[1]
    assert heads % hpb == 0 and ls % SUBLANES == 0
    assert all(lk <= lv and lk % SUBLANES == 0 and lk > ls for lk in lks)
    dvb = hpb * dv
    qspec = lambda col, g: pl.BlockSpec((ls, LANES), lambda b, h: (b, (col + g * nqk) // LANES + h))
    cspec = lambda lk: pl.BlockSpec((1, lk, LANES), lambda b, h: (b, 0, h))
    vcspec = pl.BlockSpec((1, lv, dvb), lambda b, h: (b, 0, h))
    kc2 = [kc.reshape(bsz, lk, nqk) for kc, lk in zip(k_caches, lks)]
    vc2 = v_cache.reshape(bsz, lv, heads * dv)
    outs = pl.pallas_call(
        functools.partial(_dil_sample_body, wins=wins, dils=dils, dk=dk),
        grid=(bsz, heads // hpb),
        in_specs=[qspec(col_q, 0), qspec(col_q, 1), qspec(col_q, 2), qspec(col_k, 0), qspec(col_k, 1), qspec(col_k, 2),
                  pl.BlockSpec((ls, dvb), lambda b, h: (b, col_v // dvb + h)),
                  cspec(lks[0]), cspec(lks[1]), cspec(lks[2]), vcspec],
        out_specs=[pl.BlockSpec((ls, dvb), lambda b, h: (b, h)), cspec(lks[0]), cspec(lks[1]), cspec(lks[2]), vcspec],
        out_shape=[jax.ShapeDtypeStruct((bsz * ls, heads * dv), F32)]
        + [jax.ShapeDtypeStruct(a.shape, F32) for a in kc2] + [jax.ShapeDtypeStruct(vc2.shape, F32)],
        compiler_params=_cparams("arbitrary", "arbitrary"),
        name="dilated_sample",
    )(*([proj2d] * 7), *kc2, vc2)
    o, nk0, nk1, nk2, nv = outs
    new_k = tuple(a.reshape(kc.shape) for a, kc in zip((nk0, nk1, nk2), k_caches))
    return o, new_k, nv.reshape(v_cache.shape)


def _mlstm_chunks(seq):
    if seq % MLSTM_CHUNK == 0:
        return MLSTM_CHUNK, MLSTM_CHUNK
    return seq, max(MLSTM_MIN_CHUNK, seq)


def _chunk_of(seq, chunk):
    return chunk if seq % chunk == 0 else seq


def kernel(x_prompt, x_sample, state_mlstm_c, state_mlstm_n, state_mlstm_m, state_mlstm_conv, state_s5_re, state_s5_im, cache_dil_k1, cache_dil_k2, cache_dil_k3, cache_dil_v, state_pool, norm_g, final_norm_g, a_w_in, a_b_gate, a_conv_w, a_conv_b, a_w_q, a_w_k, a_w_v, a_norm_g, a_skip, a_w_out, b_w_in, b_lam_re, b_lam_im, b_log_dt, b_B_re, b_B_im, b_C_re, b_C_im, b_d, b_w_glu, b_w_out, c_w_in, c_w_out, d_w_in, d_w_grp, d_scale, d_w_out):
    bp, lp, dm = x_prompt.shape
    bs, ls, _ = x_sample.shape
    depth = norm_g.shape[0]
    H, Dh = a_w_q.shape[1], a_w_q.shape[2]
    E = H * Dh
    heads, dk = cache_dil_k1.shape[3], cache_dil_k1.shape[4]
    dv = cache_dil_v.shape[4]
    nqk = heads * dk
    npat = len(DIL_PATTERNS)
    names = ('mlstm_c', 'mlstm_n', 'mlstm_m', 'mlstm_conv', 's5_re', 's5_im', 'k1', 'k2', 'k3', 'v', 'pool')
    new_p = {nm: [] for nm in names}
    new_s = {nm: [] for nm in names}
    yp = x_prompt.reshape(bp * lp, dm)
    ys = x_sample.reshape(bs * ls, dm)
    zeros = lambda *shape: jnp.zeros(shape, F32)

    def proj_both(layer, w):
        wb = w.astype(BF16)
        return (_norm_matmul(yp, norm_g[layer], wb, PROJ_TM, PROJ_TN),
                _norm_matmul(ys, norm_g[layer], wb, PROJ_TM, PROJ_TN))

    for layer in range(depth):
        kind, j = layer % N_MIXERS, layer // N_MIXERS
        fg = final_norm_g if layer == depth - 1 else None
        if kind == 0:
            pp, ps = proj_both(layer, a_w_in[j][:, :3 * E])
            gp, gs = proj_both(layer, jnp.pad(a_w_in[j][:, 3 * E:], ((0, 0), (0, LANES - 2 * H))))
            w = (a_b_gate[j], a_conv_w[j], a_conv_b[j], a_w_q[j], a_w_k[j], a_w_v[j], a_norm_g[j], a_skip[j])
            ap, *sp = _mlstm_pallas(pp, gp, bp, lp, *_mlstm_chunks(lp), zeros(bp, a_conv_w.shape[1] - 1, E),
                                    zeros(bp, H, Dh, Dh), zeros(bp, H, Dh), zeros(bp, H), *w)
            as_, *ss = _mlstm_pallas(ps, gs, bs, ls, *_mlstm_chunks(ls), state_mlstm_conv[j], state_mlstm_c[j],
                                     state_mlstm_n[j], state_mlstm_m[j], *w)
            keys = ('mlstm_conv', 'mlstm_c', 'mlstm_n', 'mlstm_m')
            zblk, w_out = 1, a_w_out[j]
        elif kind == 1:
            pp, ps = proj_both(layer, b_w_in[j])
            a_re, a_im, bb_re, bb_im = _s5_discretise_pallas(b_lam_re[j], b_lam_im[j], b_log_dt[j], b_B_re[j], b_B_im[j])
            w = (a_re, a_im, bb_re, bb_im, b_C_re[j], b_C_im[j], b_d[j])
            zs = zeros(bp, *state_s5_re.shape[2:])
            gp_, *sp = _s5_pallas(pp, bp, lp, _chunk_of(lp, S5_CHUNK), zs, zs, *w)
            gs_, *ss = _s5_pallas(ps, bs, ls, _chunk_of(ls, S5_CHUNK), state_s5_re[j], state_s5_im[j], *w)
            wglu = b_w_glu[j].astype(BF16)
            ap, as_ = _glu(gp_, wglu, GLU_TM, GLU_TN), _glu(gs_, wglu, GLU_TM, GLU_TN)
            keys = ('s5_re', 's5_im')
            zblk, w_out = 1, b_w_out[j]
        elif kind == 2:
            wc = c_w_in[j]
            wc = jnp.concatenate([wc[:, 2 * npat * nqk + E:], wc[:, 2 * npat * nqk:2 * npat * nqk + E],
                                  wc[:, :2 * npat * nqk]], axis=1)
            col_v, col_q, col_k = E, 2 * E, 2 * E + npat * nqk
            pp, ps = proj_both(layer, wc)
            ap = _dil_prompt_pallas(pp, bp, lp, heads, dk, dv, col_q, col_k, col_v)
            pp3 = pp.reshape(bp, lp, -1)
            sp = [pp3[:, lp - min(win, lp):, col_k + g * nqk:col_k + (g + 1) * nqk].reshape(bp, -1, heads, dk)
                  for g, (win, _) in enumerate(DIL_PATTERNS)]
            sp.append(pp3[:, lp - min(DIL_PATTERNS[-1][0], lp):, col_v:col_v + E].reshape(bp, -1, heads, dv))
            as_, kq, vq = _dil_sample_pallas(ps, bs, ls, (cache_dil_k1[j], cache_dil_k2[j], cache_dil_k3[j]),
                                             cache_dil_v[j], col_q, col_k, col_v)
            ss = (*kq, vq)
            keys = ('k1', 'k2', 'k3', 'v')
            zblk, w_out = 0, c_w_out[j]
        else:
            pp, ps = proj_both(layer, d_w_in[j])
            dp = max(POOL_WINDOWS) - 1
            ap, *sp = _pool_pallas(pp, bp, lp, _chunk_of(lp, POOL_CHUNK), zeros(bp, dp, E), 0, d_w_grp[j], d_scale[j],
                                   POOL_WINDOWS)
            as_, *ss = _pool_pallas(ps, bs, ls, _chunk_of(ls, POOL_CHUNK), state_pool[j], PAST_LEN, d_w_grp[j],
                                    d_scale[j], POOL_WINDOWS)
            keys = ('pool',)
            zblk, w_out = 1, d_w_out[j]
        for nm, a, b in zip(keys, sp, ss):
            new_p[nm].append(a)
            new_s[nm].append(b)
        wo = w_out.astype(BF16)
        yp = _gated_out(yp, (ap, 0), (pp, zblk), wo, OUT_TM, fg)
        ys = _gated_out(ys, (as_, 0), (ps, zblk), wo, OUT_TM, fg)
    out = [yp.reshape(bp, lp, dm), ys.reshape(bs, ls, dm)]
    for nm in names:
        out.append(jnp.stack(new_p[nm]))
        out.append(jnp.stack(new_s[nm]))
    return tuple(out)
```

```python
import functools
import math

import jax
import jax.numpy as jnp
from jax import lax
from jax.experimental import pallas as pl
from jax.experimental.pallas import tpu as pltpu

F32 = jnp.float32
BF16 = jnp.bfloat16
RMS_EPS = 1e-6
HEAD_NORM_EPS = 1e-6
LANES = 128
SUBLANES = 8
VMEM_LIMIT = 48 * 1024 * 1024
PAST_LEN = 8192
N_MIXERS = 4
PROJ_TM, PROJ_TN = 1024, 1024
OUT_TM = 512
GLU_TM, GLU_TN = 512, 1024


def _cparams(*sem):
    return pltpu.CompilerParams(dimension_semantics=sem, vmem_limit_bytes=VMEM_LIMIT)


def _sigmoid(x):
    return 1.0 / (1.0 + jnp.exp(-x))


def _dot_nt(a, b):
    return lax.dot_general(a, b, (((1,), (1,)), ((), ())), preferred_element_type=F32)


def _norm_matmul_body(x_ref, g_ref, w_ref, o_ref, h_scr):
    @pl.when(pl.program_id(1) == 0)
    def _():
        xf = x_ref[...]
        ms = jnp.mean(xf * xf, axis=-1, keepdims=True)
        h_scr[...] = (xf * lax.rsqrt(ms + RMS_EPS) * g_ref[...]).astype(BF16)

    o_ref[...] = jnp.dot(h_scr[...], w_ref[...], preferred_element_type=F32)


def _norm_matmul(x2d, g, w_bf16, tm, tn):
    m, k = x2d.shape
    n = w_bf16.shape[1]
    tm, tn = min(tm, m), min(tn, n)
    assert m % tm == 0 and n % tn == 0
    return pl.pallas_call(
        _norm_matmul_body,
        grid=(m // tm, n // tn),
        in_specs=[pl.BlockSpec((tm, k), lambda i, j: (i, 0)),
                  pl.BlockSpec((1, k), lambda i, j: (0, 0)),
                  pl.BlockSpec((k, tn), lambda i, j: (0, j))],
        out_specs=pl.BlockSpec((tm, tn), lambda i, j: (i, j)),
        out_shape=jax.ShapeDtypeStruct((m, n), F32),
        scratch_shapes=[pltpu.VMEM((tm, k), BF16)],
        compiler_params=_cparams("arbitrary", "arbitrary"),
        name="norm_matmul",
    )(x2d, g.reshape(1, k), w_bf16)


def _gated_out_body(*refs, final):
    if final:
        x_ref, a_ref, z_ref, w_ref, fg_ref, o_ref = refs
    else:
        x_ref, a_ref, z_ref, w_ref, o_ref = refs
    z = z_ref[...]
    act = (a_ref[...] * (z * _sigmoid(z))).astype(BF16)
    y = x_ref[...] + jnp.dot(act, w_ref[...], preferred_element_type=F32)
    if final:
        ms = jnp.mean(y * y, axis=-1, keepdims=True)
        y = y * lax.rsqrt(ms + RMS_EPS) * fg_ref[...]
    o_ref[...] = y


def _gated_out(x2d, a_src, z_src, w_bf16, tm, final_g=None):
    m, d = x2d.shape
    e = w_bf16.shape[0]
    tm = min(tm, m)
    assert m % tm == 0
    (a_arr, a_blk), (z_arr, z_blk) = a_src, z_src
    in_specs = [pl.BlockSpec((tm, d), lambda i: (i, 0)),
                pl.BlockSpec((tm, e), lambda i: (i, a_blk)),
                pl.BlockSpec((tm, e), lambda i: (i, z_blk)),
                pl.BlockSpec((e, d), lambda i: (0, 0))]
    args = [x2d, a_arr, z_arr, w_bf16]
    if final_g is not None:
        in_specs.append(pl.BlockSpec((1, d), lambda i: (0, 0)))
        args.append(final_g.reshape(1, d))
    return pl.pallas_call(
        functools.partial(_gated_out_body, final=final_g is not None),
        grid=(m // tm,),
        in_specs=in_specs,
        out_specs=pl.BlockSpec((tm, d), lambda i: (i, 0)),
        out_shape=jax.ShapeDtypeStruct((m, d), F32),
        compiler_params=_cparams("arbitrary"),
        name="gated_out",
    )(*args)


MLSTM_CHUNK = 256
MLSTM_MIN_CHUNK = 128
CONV_HALO = SUBLANES


def _log_sigmoid(x):
    return jnp.minimum(x, 0.0) - jnp.log1p(jnp.exp(-jnp.abs(x)))


def _mlstm_body(xm_ref, op_ref, gt_ref, bg_ref, cw_ref, cb_ref, wq_ref, wk_ref, wv_ref, ng_ref, sk_ref,
                cp_ref, c0_ref, n0_ref, m0_ref,
                hn_ref, cs_ref, c_ref, n_ref, m_ref, ext, *, t_in, t, heads):
    h, c = pl.program_id(0), pl.program_id(2)
    kw = cw_ref.shape[0]
    dh = xm_ref.shape[1]
    lo = CONV_HALO - (kw - 1)

    @pl.when(c == 0)
    def _():
        c_ref[...] = c0_ref[...]
        n_ref[...] = n0_ref[...]
        m_ref[...] = m0_ref[...]
        ext[lo:CONV_HALO, :] = cp_ref[0]
        if t_in < t:
            ext[CONV_HALO + t_in:CONV_HALO + t, :] = jnp.zeros((t - t_in, dh), F32)

    @pl.when(c > 0)
    def _():
        ext[lo:CONV_HALO, :] = ext[lo + t_in:CONV_HALO + t_in, :]

    ext[CONV_HALO:CONV_HALO + t_in, :] = xm_ref[...]
    cs_ref[0] = ext[lo + t_in:CONV_HALO + t_in, :]

    xm = ext[CONV_HALO:CONV_HALO + t, :]
    xconv = cb_ref[...] + ext[lo:lo + t, :] * cw_ref[0:1, :]
    for i in range(1, kw):
        xconv = xconv + ext[lo + i:lo + i + t, :] * cw_ref[i:i + 1, :]
    xc = xconv * _sigmoid(xconv)
    xcb = xc.astype(BF16)
    q = jnp.dot(xcb, wq_ref[0], preferred_element_type=F32)
    k = jnp.dot(xcb, wk_ref[0], preferred_element_type=F32) * (dh ** -0.5)
    v = jnp.dot(xm.astype(BF16), wv_ref[0], preferred_element_type=F32)

    gt = gt_ref[...] + bg_ref[...]
    row = lax.broadcasted_iota(jnp.int32, (t, 1), 0)
    if t_in < t:
        gt = jnp.concatenate([gt, jnp.zeros((t - t_in, gt.shape[1]), F32)], axis=0)
        valid = row < t_in
        ig_all = jnp.where(valid, gt, -jnp.inf)
        f_all = jnp.where(valid, _log_sigmoid(gt), 0.0)
    else:
        ig_all = gt
        f_all = _log_sigmoid(gt)
    sh = 1
    while sh < t:
        f_all = f_all + jnp.where(row >= sh, pltpu.roll(f_all, sh, 0), 0.0)
        sh *= 2
    lane = lax.broadcasted_iota(jnp.int32, (1, gt.shape[1]), 1)
    sub = lax.broadcasted_iota(jnp.int32, (gt.shape[1], 1), 0)
    col_of = lambda x, idx: jnp.sum(jnp.where(lane == idx, x, 0.0), axis=1, keepdims=True)
    row_of = lambda xt, idx: jnp.sum(jnp.where(sub == idx, xt, 0.0), axis=0, keepdims=True)
    f_col, ig_col = col_of(f_all, heads + h), col_of(ig_all, h)
    f_row, ig_row = row_of(f_all.T, heads + h), row_of(ig_all.T, h)

    m = m_ref[0]
    colidx = lax.broadcasted_iota(jnp.int32, (1, t), 1)
    dlog = jnp.where(row >= colidx, f_col - f_row + ig_row, -jnp.inf)
    inter = f_col + m
    mt = jnp.maximum(inter, jnp.max(dlog, axis=1, keepdims=True))
    w = jnp.exp(dlog - mt)
    a = jnp.exp(inter - mt)
    qb, kb, vb = q.astype(BF16), k.astype(BF16), v.astype(BF16)
    s = _dot_nt(qb, kb) * w
    cmat = c_ref[0]
    num = a * jnp.dot(qb, cmat.astype(BF16), preferred_element_type=F32) \
        + jnp.dot(s.astype(BF16), vb, preferred_element_type=F32)
    nvec = n_ref[0]
    den = a * jnp.sum(q * nvec, axis=1, keepdims=True) + jnp.sum(s, axis=1, keepdims=True)
    hloc = num / jnp.maximum(jnp.abs(den), jnp.exp(-mt))
    m_new = mt[t - 1:t, :]
    f_tot = f_col[t - 1:t, :]
    decay = jnp.exp(f_tot + m - m_new)
    ws = jnp.exp(f_tot - f_col + ig_col - m_new)
    kws = k * ws
    c_ref[0] = decay * cmat + lax.dot_general(kws.astype(BF16), vb, (((0,), (0,)), ((), ())),
                                              preferred_element_type=F32)
    n_ref[0] = decay * nvec + jnp.sum(kws, axis=0, keepdims=True)
    m_ref[0] = m_new

    o = hloc[:t_in, :] * _sigmoid(op_ref[...])
    mu = jnp.mean(o, axis=1, keepdims=True)
    var = jnp.mean(jnp.square(o - mu), axis=1, keepdims=True)
    hn = (o - mu) * lax.rsqrt(var + HEAD_NORM_EPS)
    hn_ref[...] = hn * ng_ref[...] + sk_ref[...] * xc[:t_in, :]


def _mlstm_pallas(proj2d, gates2d, bsz, seq, t_in, t, conv_prev, c0, n0, m0, layer, b_gate, conv_w, conv_b,
                  w_q, w_k, w_v, norm_g, skip):
    heads, dh, _ = w_q.shape
    e = heads * dh
    kw = conv_w.shape[0]
    gl = gates2d.shape[1]
    nl = c0.shape[0]
    assert seq % t_in == 0 and t_in <= t and t_in % SUBLANES == 0 and kw - 1 <= min(CONV_HALO, t_in)
    nc = seq // t_in
    rows = lambda blk: pl.BlockSpec((t_in, dh), lambda h, b, c: (b * nc + c, blk(h)))
    per_head_vec = lambda r: pl.BlockSpec((r, dh), lambda h, b, c: (0, h))
    wspec = pl.BlockSpec((1, dh, dh), lambda h, b, c: (h, 0, 0))
    st = lambda shp, off: pl.BlockSpec((1,) + shp, lambda h, b, c: (off + b * heads + h, 0, 0))
    cpspec = lambda off: pl.BlockSpec((1, kw - 1, dh), lambda h, b, c: (off + b, 0, h))
    sbase = layer * bsz * heads
    hn, cs, c_new, n_new, m_new = pl.pallas_call(
        functools.partial(_mlstm_body, t_in=t_in, t=t, heads=heads),
        grid=(heads, bsz, nc),
        in_specs=[rows(lambda h: h), rows(lambda h: 2 * heads + h),
                  pl.BlockSpec((t_in, gl), lambda h, b, c: (b * nc + c, 0)),
                  pl.BlockSpec((1, gl), lambda h, b, c: (0, 0)),
                  per_head_vec(kw), per_head_vec(1), wspec, wspec, wspec, per_head_vec(1), per_head_vec(1),
                  cpspec(layer * bsz), st((dh, dh), sbase), st((1, dh), sbase), st((1, 1), sbase)],
        out_specs=[rows(lambda h: h), cpspec(0), st((dh, dh), 0), st((1, dh), 0), st((1, 1), 0)],
        out_shape=[jax.ShapeDtypeStruct((bsz * seq, e), F32), jax.ShapeDtypeStruct((bsz, kw - 1, e), F32),
                   jax.ShapeDtypeStruct((bsz * heads, dh, dh), F32), jax.ShapeDtypeStruct((bsz * heads, 1, dh), F32),
                   jax.ShapeDtypeStruct((bsz * heads, 1, 1), F32)],
        scratch_shapes=[pltpu.VMEM((CONV_HALO + t, dh), F32)],
        compiler_params=_cparams("arbitrary", "arbitrary", "arbitrary"),
        name="mlstm",
    )(proj2d, proj2d, gates2d, jnp.pad(b_gate, (0, gl - b_gate.shape[0])).reshape(1, gl),
      conv_w, conv_b.reshape(1, e), w_q.astype(BF16), w_k.astype(BF16), w_v.astype(BF16),
      norm_g.reshape(1, e), skip.reshape(1, e), conv_prev.reshape(nl * bsz, kw - 1, e),
      c0.reshape(nl * bsz * heads, dh, dh), n0.reshape(nl * bsz * heads, 1, dh), m0.reshape(nl * bsz * heads, 1, 1))
    return (hn, cs, c_new.reshape(bsz, heads, dh, dh), n_new.reshape(bsz, heads, dh), m_new.reshape(bsz, heads))


def _s5_disc_body(lr_ref, li_ref, ldt_ref, br_ref, bi_ref, ar_ref, ai_ref, bbr_ref, bbi_ref):
    lr = jnp.minimum(lr_ref[...], -1e-4)
    li = li_ref[...]
    dt = jnp.exp(ldt_ref[...])
    mag = jnp.exp(dt * lr)
    a_re, a_im = mag * jnp.cos(dt * li), mag * jnp.sin(dt * li)
    den = lr * lr + li * li
    xr, xi = a_re - 1.0, a_im
    cr = (xr * lr + xi * li) / den
    ci = (xi * lr - xr * li) / den
    ar_ref[...] = a_re
    ai_ref[...] = a_im
    b_r, b_i = br_ref[...], bi_ref[...]
    bbr_ref[...] = cr * b_r - ci * b_i
    bbi_ref[...] = cr * b_i + ci * b_r


def _s5_discretise_pallas(lam_re, lam_im, log_dt, b_re, b_im):
    g, p, c = b_re.shape
    vm = pl.BlockSpec(memory_space=pltpu.VMEM)
    a_re, a_im, bb_re, bb_im = pl.pallas_call(
        _s5_disc_body,
        in_specs=[vm] * 5,
        out_specs=[vm] * 4,
        out_shape=[jax.ShapeDtypeStruct((g, 1, p), F32)] * 2 + [jax.ShapeDtypeStruct((g, c, p), F32)] * 2,
        name="s5_discretise",
    )(lam_re.reshape(g, 1, p), lam_im.reshape(g, 1, p), log_dt.reshape(g, 1, 1),
      jnp.swapaxes(b_re, 1, 2), jnp.swapaxes(b_im, 1, 2))
    return a_re.reshape(g, p), a_im.reshape(g, p), bb_re, bb_im


S5_STRIP_GROUPS = 8


def _gelu_tanh(x):
    return x * (0.5 * (1.0 + jnp.tanh(math.sqrt(2.0 / math.pi) * (x + 0.044715 * (x * x * x)))))


S5_SEGS = SUBLANES
S5_SEG_CHUNK = 512
S5_UNROLL = 64


def _cmul(ar, ai, br, bi):
    return ar * br - ai * bi, ar * bi + ai * br


def _s5_seg_body(u_ref, wbr_ref, wbi_ref, wcr_ref, wci_ref, ar_ref, ai_ref, d_ref, h0r_ref, h0i_ref,
                 g_ref, hr_ref, hi_ref, xr_scr, xi_scr, pwr_scr, pwi_scr, apr_scr, api_scr, *, chained):
    s = pl.program_id(2)
    nseg = S5_SEGS
    rows = u_ref.shape[0]
    jn = rows // nseg
    ar, ai = ar_ref[s], ai_ref[s]

    @pl.when((pl.program_id(0) == 0) & (pl.program_id(1) == 0))
    def _():
        pr, pi = ar, ai
        for j in range(jn):
            pwr_scr[s, j:j + 1, :] = pr
            pwi_scr[s, j:j + 1, :] = pi
            if j + 1 < jn:
                pr, pi = _cmul(pr, pi, ar, ai)
        qr, qi = jnp.ones_like(ar), jnp.zeros_like(ai)
        for k in range(nseg + 1):
            apr_scr[s, k:k + 1, :] = qr
            api_scr[s, k:k + 1, :] = qi
            qr, qi = _cmul(qr, qi, pr, pi)

    if chained:
        @pl.when(pl.program_id(1) == 0)
        def _():
            hr_ref[s] = h0r_ref[s]
            hi_ref[s] = h0i_ref[s]

    up = jnp.concatenate([u_ref[pl.ds(j, nseg, stride=jn), :] for j in range(jn)], axis=0)
    ub = up.astype(BF16)
    xr_scr[...] = jnp.dot(ub, wbr_ref[s], preferred_element_type=F32)
    xi_scr[...] = jnp.dot(ub, wbi_ref[s], preferred_element_type=F32)
    blk_of = lambda j: pl.ds(pl.multiple_of(j * nseg, nseg), nseg)

    def scan_step(j, h):
        tr, ti = _cmul(ar, ai, h[0], h[1])
        hr, hi = tr + xr_scr[blk_of(j), :], ti + xi_scr[blk_of(j), :]
        xr_scr[blk_of(j), :] = hr
        xi_scr[blk_of(j), :] = hi
        return hr, hi

    zero = jnp.zeros((nseg, ar.shape[1]), F32)
    hr, hi = lax.fori_loop(0, jn, scan_step, (zero, zero), unroll=min(jn, S5_UNROLL))
    anr, ani = pwr_scr[s, jn - 1:jn, :], pwi_scr[s, jn - 1:jn, :]
    if chained:
        sub = lax.broadcasted_iota(jnp.int32, (nseg, 1), 0)
        er, ei, pr, pi = hr, hi, anr, ani
        sh = 1
        while sh < nseg:
            keep = sub >= sh
            tr, ti = _cmul(pr, pi, jnp.where(keep, pltpu.roll(er, sh, 0), 0.0),
                           jnp.where(keep, pltpu.roll(ei, sh, 0), 0.0))
            er, ei = er + tr, ei + ti
            pr, pi = _cmul(pr, pi, pr, pi)
            sh *= 2
        hin_r, hin_i = hr_ref[s], hi_ref[s]
        tr, ti = _cmul(apr_scr[s, 0:nseg, :], api_scr[s, 0:nseg, :], hin_r, hin_i)
        cr = jnp.where(sub >= 1, pltpu.roll(er, 1, 0), 0.0) + tr
        ci = jnp.where(sub >= 1, pltpu.roll(ei, 1, 0), 0.0) + ti
        tr, ti = _cmul(apr_scr[s, 1:nseg + 1, :], api_scr[s, 1:nseg + 1, :], hin_r, hin_i)
        hr_ref[s] = (er + tr)[nseg - 1:nseg, :]
        hi_ref[s] = (ei + ti)[nseg - 1:nseg, :]
    else:
        cr, ci = h0r_ref[...], h0i_ref[...]
        tr, ti = _cmul(anr, ani, cr, ci)
        hr_ref[...] = hr + tr
        hi_ref[...] = hi + ti
    def fix_step(j, carry):
        tr, ti = _cmul(pwr_scr[s, pl.ds(j, 1), :], pwi_scr[s, pl.ds(j, 1), :], cr, ci)
        xr_scr[blk_of(j), :] = xr_scr[blk_of(j), :] + tr
        xi_scr[blk_of(j), :] = xi_scr[blk_of(j), :] + ti
        return carry

    lax.fori_loop(0, jn, fix_step, 0, unroll=min(jn, S5_UNROLL))
    y = (jnp.dot(xr_scr[...].astype(BF16), wcr_ref[s], preferred_element_type=F32)
         - jnp.dot(xi_scr[...].astype(BF16), wci_ref[s], preferred_element_type=F32))
    g = _gelu_tanh(y + d_ref[...] * up)
    for j in range(jn):
        g_ref[pl.ds(j, nseg, stride=jn), :] = g[j * nseg:(j + 1) * nseg, :]


def _s5_seg_pallas(proj2d, bsz, seq, chunk, h0_re, h0_im, a_re, a_im, bb_re, bb_im, c_re, c_im, d_skip):
    g, cch, p = bb_re.shape
    e = g * cch
    sg = S5_STRIP_GROUPS
    ns = g // sg
    lu, ls = sg * cch, sg * p
    assert g % sg == 0 and lu == LANES
    chained = seq % chunk == 0
    if chained:
        rows, n0, nc = chunk, bsz, seq // chunk
    else:
        assert bsz % S5_SEGS == 0
        rows, n0, nc = S5_SEGS * seq, bsz // S5_SEGS, 1
    jn = rows // S5_SEGS
    eye = jnp.eye(sg, dtype=F32)
    wb = lambda bb: jnp.einsum('sgcp,gh->sgchp', bb.reshape(ns, sg, cch, p), eye).reshape(ns, lu, ls).astype(BF16)
    wc = lambda cc: jnp.einsum('sgcp,gh->sgphc', cc.reshape(ns, sg, cch, p), eye).reshape(ns, ls, lu).astype(BF16)
    const3 = lambda shp: pl.BlockSpec(shp, lambda b, c, s: (0, 0, 0))
    if chained:
        hspec = pl.BlockSpec((ns, 1, ls), lambda b, c, s: (b, 0, 0))
        hshape = (bsz * ns, 1, ls)
    else:
        hspec = pl.BlockSpec((S5_SEGS, ls), lambda b, c, s: (b, s))
        hshape = (bsz, ns * ls)
    uspec = pl.BlockSpec((rows, lu), lambda b, c, s: (b * nc + c, s))
    gout, hr, hi = pl.pallas_call(
        functools.partial(_s5_seg_body, chained=chained),
        grid=(n0, nc, ns),
        in_specs=[uspec, const3((ns, lu, ls)), const3((ns, lu, ls)), const3((ns, ls, lu)), const3((ns, ls, lu)),
                  const3((ns, 1, ls)), const3((ns, 1, ls)),
                  pl.BlockSpec((1, lu), lambda b, c, s: (0, s)), hspec, hspec],
        out_specs=[uspec, hspec, hspec],
        out_shape=[jax.ShapeDtypeStruct((bsz * seq, e), F32),
                   jax.ShapeDtypeStruct(hshape, F32), jax.ShapeDtypeStruct(hshape, F32)],
        scratch_shapes=[pltpu.VMEM((rows, ls), F32), pltpu.VMEM((rows, ls), F32),
                        pltpu.VMEM((ns, jn, ls), F32), pltpu.VMEM((ns, jn, ls), F32),
                        pltpu.VMEM((ns, 2 * S5_SEGS, ls), F32), pltpu.VMEM((ns, 2 * S5_SEGS, ls), F32)],
        compiler_params=_cparams("arbitrary", "arbitrary", "arbitrary"),
        name="s5_scan",
    )(proj2d, wb(bb_re), wb(bb_im), wc(c_re), wc(c_im),
      a_re.reshape(ns, 1, ls), a_im.reshape(ns, 1, ls), d_skip.reshape(1, e),
      h0_re.reshape(hshape), h0_im.reshape(hshape))
    return gout, hr.reshape(bsz, g, p), hi.reshape(bsz, g, p)


def _glu_body(g_ref, gc_ref, w_ref, o_ref, gb_scr):
    @pl.when(pl.program_id(1) == 0)
    def _():
        gb_scr[...] = g_ref[...].astype(BF16)

    o_ref[...] = gc_ref[...] * _sigmoid(jnp.dot(gb_scr[...], w_ref[...], preferred_element_type=F32))


def _glu(g2d, w_bf16, tm, tn):
    m, e = g2d.shape
    tm, tn = min(tm, m), min(tn, e)
    assert m % tm == 0 and e % tn == 0
    return pl.pallas_call(
        _glu_body,
        grid=(m // tm, e // tn),
        in_specs=[pl.BlockSpec((tm, e), lambda i, j: (i, 0)),
                  pl.BlockSpec((tm, tn), lambda i, j: (i, j)),
                  pl.BlockSpec((e, tn), lambda i, j: (0, j))],
        out_specs=pl.BlockSpec((tm, tn), lambda i, j: (i, j)),
        out_shape=jax.ShapeDtypeStruct((m, e), F32),
        scratch_shapes=[pltpu.VMEM((tm, e), BF16)],
        compiler_params=_cparams("arbitrary", "arbitrary"),
        name="glu",
    )(g2d, g2d, w_bf16)


POOL_WINDOWS = (2, 4, 8, 16)
POOL_HALO = 2 * SUBLANES
POOL_CHUNK = 512


def _pool_body(u_ref, pre_ref, w_ref, sc_ref, o_ref, st_ref, ext, *, windows, start):
    c = pl.program_id(1)
    t = u_ref.shape[0]
    dp = max(windows) - 1
    dg = u_ref.shape[1] // len(windows)

    @pl.when(c == 0)
    def _():
        ext[POOL_HALO - dp:POOL_HALO, :] = pre_ref[0]

    @pl.when(c > 0)
    def _():
        ext[POOL_HALO - dp:POOL_HALO, :] = ext[POOL_HALO + t - dp:POOL_HALO + t, :]

    ext[POOL_HALO:POOL_HALO + t, :] = u_ref[...]
    st_ref[0] = ext[POOL_HALO + t - dp:POOL_HALO + t, :]
    pos = start + c * t + lax.broadcasted_iota(jnp.int32, (t, 1), 0)
    for g, w in enumerate(windows):
        lo = g * dg
        cur = ext[POOL_HALO:POOL_HALO + t, lo:lo + dg]
        tot = cur
        for j in range(1, w):
            tot = tot + ext[POOL_HALO - j:POOL_HALO - j + t, lo:lo + dg]
        cnt = jnp.minimum(pos + 1, w).astype(F32)
        mix = (tot / cnt - cur).astype(BF16)
        o_ref[:, lo:lo + dg] = jnp.dot(mix, w_ref[g], preferred_element_type=F32) * sc_ref[:, lo:lo + dg]


def _pool_pallas(proj2d, bsz, seq, chunk, prefix, start, w_grp, scale, windows):
    nw, dg, _ = w_grp.shape
    e = nw * dg
    dp = max(windows) - 1
    assert seq % chunk == 0 and prefix.shape == (bsz, dp, e) and dp < POOL_HALO and nw == len(windows)
    nc = seq // chunk
    return pl.pallas_call(
        functools.partial(_pool_body, windows=windows, start=start),
        grid=(bsz, nc),
        in_specs=[pl.BlockSpec((chunk, e), lambda b, c: (b * nc + c, 0)),
                  pl.BlockSpec((1, dp, e), lambda b, c: (b, 0, 0)),
                  pl.BlockSpec((nw, dg, dg), lambda b, c: (0, 0, 0)),
                  pl.BlockSpec((1, e), lambda b, c: (0, 0))],
        out_specs=[pl.BlockSpec((chunk, e), lambda b, c: (b * nc + c, 0)),
                   pl.BlockSpec((1, dp, e), lambda b, c: (b, 0, 0))],
        out_shape=[jax.ShapeDtypeStruct((bsz * seq, e), F32), jax.ShapeDtypeStruct((bsz, dp, e), F32)],
        scratch_shapes=[pltpu.VMEM((POOL_HALO + chunk, e), F32)],
        compiler_params=_cparams("arbitrary", "arbitrary"),
        name="pool",
    )(proj2d, prefix, w_grp.astype(BF16), scale.reshape(1, e))


DIL_PATTERNS = ((128, 1), (512, 4), (2048, 16))
DIL_BLOCK = 128
DIL_TQ = 2048
DIL_UNROLL = 8
DIL_MERGE_ROWS = 256


def _dil_prompt_body(*refs, dils, dk, nvp):
    q0_ref, q1_ref, q2_ref, k0_ref, k1_ref, k2_ref = refs[:6]
    v_refs = refs[6:6 + nvp]
    o_ref = refs[6 + nvp]
    per_pat = nvp + 2
    scr = refs[7 + nvp:]
    pat_scr = [scr[g * per_pat:(g + 1) * per_pat] for g in range(len(dils))]
    h, i = pl.program_id(1), pl.program_id(2)
    tq = o_ref.shape[0]
    qb = DIL_BLOCK
    nsub = tq // qb
    lane = lax.broadcasted_iota(jnp.int32, (1, LANES), 1)
    mine = (lane // dk) == (h % (LANES // dk))
    rowi = lax.broadcasted_iota(jnp.int32, (qb, 1), 0)
    coli = lax.broadcasted_iota(jnp.int32, (1, qb), 1)
    scale = dk ** -0.5
    for g, d in enumerate(dils):
        q_ref, k_ref = (q0_ref, q1_ref, q2_ref)[g], (k0_ref, k1_ref, k2_ref)[g]
        ld = d.bit_length() - 1

        acc, m_scr, l_scr = pat_scr[g][:nvp], pat_scr[g][nvp], pat_scr[g][nvp + 1]

        def body(idx, carry, q_ref=q_ref, k_ref=k_ref, d=d, ld=ld, acc=acc, m_scr=m_scr, l_scr=l_scr):
            r, bl = idx & (d - 1), idx >> ld
            lstart = r + (d * qb) * bl
            gstart = i * tq + lstart
            has_prev = gstart >= d * qb
            pstart = jnp.where(has_prev, gstart - d * qb, gstart)
            rows = lambda s: pl.ds(s, qb, stride=d) if d > 1 else pl.ds(s, qb)
            qm = (jnp.where(mine, q_ref[rows(lstart), :], 0.0) * scale).astype(BF16)
            sc = _dot_nt(qm, k_ref[rows(gstart), :].astype(BF16))
            sp = _dot_nt(qm, k_ref[rows(pstart), :].astype(BF16))
            sc = jnp.where(coli <= rowi, sc, -jnp.inf)
            sp = jnp.where((coli >= rowi) & has_prev, sp, -jnp.inf)
            mb = jnp.max(jnp.maximum(sc, sp), axis=1, keepdims=True)
            pc, pp = jnp.exp(sc - mb), jnp.exp(sp - mb)
            lb = jnp.sum(pc + pp, axis=1, keepdims=True)
            vrows = lambda s: jnp.concatenate([v[rows(s), :] for v in v_refs], axis=1).astype(BF16)
            nb = (jnp.dot(pc.astype(BF16), vrows(gstart), preferred_element_type=F32)
                  + jnp.dot(pp.astype(BF16), vrows(pstart), preferred_element_type=F32))
            for p, a in enumerate(acc):
                a[rows(lstart), :] = nb[:, p * LANES:(p + 1) * LANES]
            m_scr[rows(lstart), :] = jnp.broadcast_to(mb, (qb, LANES))
            l_scr[rows(lstart), :] = jnp.broadcast_to(lb, (qb, LANES))
            return carry

        lax.fori_loop(0, nsub, body, 0, unroll=DIL_UNROLL)

    def merge(c, carry):
        rs = pl.ds(pl.multiple_of(c * DIL_MERGE_ROWS, DIL_MERGE_ROWS), DIL_MERGE_ROWS)
        ms = [ps[nvp][rs, :] for ps in pat_scr]
        mmax = functools.reduce(jnp.maximum, ms)
        es = [jnp.exp(m - mmax) for m in ms]
        den = sum(ps[nvp + 1][rs, :] * e for ps, e in zip(pat_scr, es))
        for p in range(nvp):
            o_ref[rs, p * LANES:(p + 1) * LANES] = sum(ps[p][rs, :] * e for ps, e in zip(pat_scr, es)) / den
        return carry

    lax.fori_loop(0, tq // DIL_MERGE_ROWS, merge, 0)


def _dil_prompt_pallas(proj2d, bsz, seq, heads, dk, dv, col_q, col_k, col_v):
    dils = tuple(d for _, d in DIL_PATTERNS)
    assert all(w == d * DIL_BLOCK for w, d in DIL_PATTERNS)
    tq = min(DIL_TQ, seq)
    assert seq % tq == 0 and tq % (DIL_BLOCK * max(dils)) == 0 and LANES % dk == 0 and dv % LANES == 0
    nq = seq // tq
    hpb = LANES // dk
    nqk = heads * dk
    nvp = dv // LANES
    qspec = lambda g: pl.BlockSpec((tq, LANES), lambda b, h, i: (b * nq + i, (col_q + g * nqk) // LANES + h // hpb))
    kspec = lambda g: pl.BlockSpec((seq, LANES), lambda b, h, i: (b, (col_k + g * nqk) // LANES + h // hpb))
    vspec = lambda p: pl.BlockSpec((seq, LANES), lambda b, h, i: (b, col_v // LANES + h * nvp + p))
    return pl.pallas_call(
        functools.partial(_dil_prompt_body, dils=dils, dk=dk, nvp=nvp),
        grid=(bsz, heads, nq),
        in_specs=[qspec(0), qspec(1), qspec(2), kspec(0), kspec(1), kspec(2)] + [vspec(p) for p in range(nvp)],
        out_specs=pl.BlockSpec((tq, dv), lambda b, h, i: (b * nq + i, h)),
        out_shape=jax.ShapeDtypeStruct((bsz * seq, heads * dv), F32),
        scratch_shapes=[pltpu.VMEM((tq, LANES), F32)] * ((2 + nvp) * len(dils)),
        compiler_params=_cparams("arbitrary", "arbitrary", "arbitrary"),
        name="dilated_prompt",
    )(*([proj2d] * (6 + nvp)))


def _dil_sample_body(q0_ref, q1_ref, q2_ref, n0_ref, n1_ref, n2_ref, vn_ref, c0_ref, c1_ref, c2_ref, vc_ref,
                     o_ref, oc0_ref, oc1_ref, oc2_ref, ovc_ref, *, wins, dils, dk):
    ls = q0_ref.shape[0]
    dvb = vn_ref.shape[1]
    hpb = LANES // dk
    lv = vc_ref.shape[1]
    lane = lax.broadcasted_iota(jnp.int32, (1, LANES), 1)
    qi = lax.broadcasted_iota(jnp.int32, (hpb * ls, 1), 0) % ls
    scale = dk ** -0.5
    vnew = vn_ref[...]
    parts = []
    for g, (win, d) in enumerate(zip(wins, dils)):
        q_ref, n_ref, c_ref = (q0_ref, q1_ref, q2_ref)[g], (n0_ref, n1_ref, n2_ref)[g], (c0_ref, c1_ref, c2_ref)[g]
        lk = c_ref.shape[1]
        qt = q_ref[...] * scale
        qbd = jnp.concatenate([jnp.where((lane // dk) == hh, qt, 0.0) for hh in range(hpb)], axis=0).astype(BF16)
        knew = n_ref[...]
        s1 = _dot_nt(qbd, c_ref[0].astype(BF16))
        s2 = _dot_nt(qbd, knew.astype(BF16))
        back1 = lk + qi - lax.broadcasted_iota(jnp.int32, (1, lk), 1)
        back2 = qi - lax.broadcasted_iota(jnp.int32, (1, ls), 1)
        s1 = jnp.where((back1 <= win) & ((back1 & (d - 1)) == 0), s1, -jnp.inf)
        s2 = jnp.where((back2 >= 0) & ((back2 & (d - 1)) == 0), s2, -jnp.inf)
        m = jnp.maximum(jnp.max(s1, axis=1, keepdims=True), jnp.max(s2, axis=1, keepdims=True))
        p1, p2 = jnp.exp(s1 - m), jnp.exp(s2 - m)
        den = jnp.sum(p1, axis=1, keepdims=True) + jnp.sum(p2, axis=1, keepdims=True)
        num = (jnp.dot(p1.astype(BF16), vc_ref[0, lv - lk:lv, :].astype(BF16), preferred_element_type=F32)
               + jnp.dot(p2.astype(BF16), vnew.astype(BF16), preferred_element_type=F32))
        parts.append((num, m, den))
        oc_ref = (oc0_ref, oc1_ref, oc2_ref)[g]
        oc_ref[0, 0:lk - ls, :] = c_ref[0, ls:lk, :]
        oc_ref[0, lk - ls:lk, :] = knew
    mmax = functools.reduce(jnp.maximum, [m for _, m, _ in parts])
    num = sum(nu * jnp.exp(m - mmax) for nu, m, _ in parts)
    den = sum(de * jnp.exp(m - mmax) for _, m, de in parts)
    o = num / den
    dvh = dvb // hpb
    o_ref[...] = jnp.concatenate([o[hh * ls:(hh + 1) * ls, hh * dvh:(hh + 1) * dvh] for hh in range(hpb)], axis=1)
    ovc_ref[0, 0:lv - ls, :] = vc_ref[0, ls:lv, :]
    ovc_ref[0, lv - ls:lv, :] = vnew


def _dil_sample_pallas(proj2d, bsz, ls, k_caches, v_cache, layer, col_q, col_k, col_v):
    heads, dk = k_caches[0].shape[3], k_caches[0].shape[4]
    dv = v_cache.shape[4]
    nl = v_cache.shape[0]
    wins, dils = tuple(w for w, _ in DIL_PATTERNS), tuple(d for _, d in DIL_PATTERNS)
    hpb = LANES // dk
    nqk = heads * dk
    lks = tuple(kc.shape[2] for kc in k_caches)
    lv = v_cache.shape[2]
    assert heads % hpb == 0 and ls % SUBLANES == 0
    assert all(lk <= lv and lk % SUBLANES == 0 and lk > ls for lk in lks)
    dvb = hpb * dv
    base = layer * bsz
    qspec = lambda col, g: pl.BlockSpec((ls, LANES), lambda b, h: (b, (col + g * nqk) // LANES + h))
    cspec = lambda lk, off: pl.BlockSpec((1, lk, LANES), lambda b, h: (off + b, 0, h))
    vcspec = lambda off: pl.BlockSpec((1, lv, dvb), lambda b, h: (off + b, 0, h))
    kc2 = [kc.reshape(nl * bsz, lk, nqk) for kc, lk in zip(k_caches, lks)]
    vc2 = v_cache.reshape(nl * bsz, lv, heads * dv)
    outs = pl.pallas_call(
        functools.partial(_dil_sample_body, wins=wins, dils=dils, dk=dk),
        grid=(bsz, heads // hpb),
        in_specs=[qspec(col_q, 0), qspec(col_q, 1), qspec(col_q, 2), qspec(col_k, 0), qspec(col_k, 1), qspec(col_k, 2),
                  pl.BlockSpec((ls, dvb), lambda b, h: (b, col_v // dvb + h)),
                  cspec(lks[0], base), cspec(lks[1], base), cspec(lks[2], base), vcspec(base)],
        out_specs=[pl.BlockSpec((ls, dvb), lambda b, h: (b, h)),
                   cspec(lks[0], 0), cspec(lks[1], 0), cspec(lks[2], 0), vcspec(0)],
        out_shape=[jax.ShapeDtypeStruct((bsz * ls, heads * dv), F32)]
        + [jax.ShapeDtypeStruct((bsz, lk, nqk), F32) for lk in lks] + [jax.ShapeDtypeStruct((bsz, lv, heads * dv), F32)],
        compiler_params=_cparams("arbitrary", "arbitrary"),
        name="dilated_sample",
    )(*([proj2d] * 7), *kc2, vc2)
    o, nk0, nk1, nk2, nv = outs
    new_k = tuple(a.reshape(bsz, lk, heads, dk) for a, lk in zip((nk0, nk1, nk2), lks))
    return o, new_k, nv.reshape(bsz, lv, heads, dv)


def _mlstm_chunks(seq):
    if seq % MLSTM_CHUNK == 0:
        return MLSTM_CHUNK, MLSTM_CHUNK
    return seq, max(MLSTM_MIN_CHUNK, seq)


def _chunk_of(seq, chunk):
    return chunk if seq % chunk == 0 else seq


def kernel(x_prompt, x_sample, state_mlstm_c, state_mlstm_n, state_mlstm_m, state_mlstm_conv, state_s5_re, state_s5_im, cache_dil_k1, cache_dil_k2, cache_dil_k3, cache_dil_v, state_pool, norm_g, final_norm_g, a_w_in, a_b_gate, a_conv_w, a_conv_b, a_w_q, a_w_k, a_w_v, a_norm_g, a_skip, a_w_out, b_w_in, b_lam_re, b_lam_im, b_log_dt, b_B_re, b_B_im, b_C_re, b_C_im, b_d, b_w_glu, b_w_out, c_w_in, c_w_out, d_w_in, d_w_grp, d_scale, d_w_out):
    bp, lp, dm = x_prompt.shape
    bs, ls, _ = x_sample.shape
    depth = norm_g.shape[0]
    H, Dh = a_w_q.shape[1], a_w_q.shape[2]
    E = H * Dh
    heads, dk = cache_dil_k1.shape[3], cache_dil_k1.shape[4]
    dv = cache_dil_v.shape[4]
    nqk = heads * dk
    npat = len(DIL_PATTERNS)
    names = ('mlstm_c', 'mlstm_n', 'mlstm_m', 'mlstm_conv', 's5_re', 's5_im', 'k1', 'k2', 'k3', 'v', 'pool')
    new_p = {nm: [] for nm in names}
    new_s = {nm: [] for nm in names}
    yp = x_prompt.reshape(bp * lp, dm)
    ys = x_sample.reshape(bs * ls, dm)
    zeros = lambda *shape: jnp.zeros(shape, F32)

    def proj_both(layer, w):
        wb = w.astype(BF16)
        return (_norm_matmul(yp, norm_g[layer], wb, PROJ_TM, PROJ_TN),
                _norm_matmul(ys, norm_g[layer], wb, PROJ_TM, PROJ_TN))

    for layer in range(depth):
        kind, j = layer % N_MIXERS, layer // N_MIXERS
        fg = final_norm_g if layer == depth - 1 else None
        if kind == 0:
            pp, ps = proj_both(layer, a_w_in[j][:, :3 * E])
            gp, gs = proj_both(layer, jnp.pad(a_w_in[j][:, 3 * E:], ((0, 0), (0, LANES - 2 * H))))
            w = (a_b_gate[j], a_conv_w[j], a_conv_b[j], a_w_q[j], a_w_k[j], a_w_v[j], a_norm_g[j], a_skip[j])
            ap, *sp = _mlstm_pallas(pp, gp, bp, lp, *_mlstm_chunks(lp), zeros(1, bp, a_conv_w.shape[1] - 1, E),
                                    zeros(1, bp, H, Dh, Dh), zeros(1, bp, H, Dh), zeros(1, bp, H), 0, *w)
            as_, *ss = _mlstm_pallas(ps, gs, bs, ls, *_mlstm_chunks(ls), state_mlstm_conv, state_mlstm_c,
                                     state_mlstm_n, state_mlstm_m, j, *w)
            keys = ('mlstm_conv', 'mlstm_c', 'mlstm_n', 'mlstm_m')
            zblk, w_out = 1, a_w_out[j]
        elif kind == 1:
            pp, ps = proj_both(layer, b_w_in[j])
            a_re, a_im, bb_re, bb_im = _s5_discretise_pallas(b_lam_re[j], b_lam_im[j], b_log_dt[j], b_B_re[j], b_B_im[j])
            w = (a_re, a_im, bb_re, bb_im, b_C_re[j], b_C_im[j], b_d[j])
            zs = zeros(bp, *state_s5_re.shape[2:])
            gp_, *sp = _s5_seg_pallas(pp, bp, lp, S5_SEG_CHUNK, zs, zs, *w)
            gs_, *ss = _s5_seg_pallas(ps, bs, ls, S5_SEG_CHUNK, state_s5_re[j], state_s5_im[j], *w)
            wglu = b_w_glu[j].astype(BF16)
            ap, as_ = _glu(gp_, wglu, GLU_TM, GLU_TN), _glu(gs_, wglu, GLU_TM, GLU_TN)
            keys = ('s5_re', 's5_im')
            zblk, w_out = 1, b_w_out[j]
        elif kind == 2:
            wc = c_w_in[j]
            wc = jnp.concatenate([wc[:, 2 * npat * nqk + E:], wc[:, 2 * npat * nqk:2 * npat * nqk + E],
                                  wc[:, :2 * npat * nqk]], axis=1)
            col_v, col_q, col_k = E, 2 * E, 2 * E + npat * nqk
            pp, ps = proj_both(layer, wc)
            ap = _dil_prompt_pallas(pp, bp, lp, heads, dk, dv, col_q, col_k, col_v)
            pp3 = pp.reshape(bp, lp, -1)
            sp = [pp3[:, lp - min(win, lp):, col_k + g * nqk:col_k + (g + 1) * nqk].reshape(bp, -1, heads, dk)
                  for g, (win, _) in enumerate(DIL_PATTERNS)]
            sp.append(pp3[:, lp - min(DIL_PATTERNS[-1][0], lp):, col_v:col_v + E].reshape(bp, -1, heads, dv))
            as_, kq, vq = _dil_sample_pallas(ps, bs, ls, (cache_dil_k1, cache_dil_k2, cache_dil_k3), cache_dil_v, j,
                                             col_q, col_k, col_v)
            ss = (*kq, vq)
            keys = ('k1', 'k2', 'k3', 'v')
            zblk, w_out = 0, c_w_out[j]
        else:
            pp, ps = proj_both(layer, d_w_in[j])
            dp = max(POOL_WINDOWS) - 1
            ap, *sp = _pool_pallas(pp, bp, lp, _chunk_of(lp, POOL_CHUNK), zeros(bp, dp, E), 0, d_w_grp[j], d_scale[j],
                                   POOL_WINDOWS)
            as_, *ss = _pool_pallas(ps, bs, ls, _chunk_of(ls, POOL_CHUNK), state_pool[j], PAST_LEN, d_w_grp[j],
                                    d_scale[j], POOL_WINDOWS)
            keys = ('pool',)
            zblk, w_out = 1, d_w_out[j]
        for nm, a, b in zip(keys, sp, ss):
            new_p[nm].append(a)
            new_s[nm].append(b)
        wo = w_out.astype(BF16)
        yp = _gated_out(yp, (ap, 0), (pp, zblk), wo, OUT_TM, fg)
        ys = _gated_out(ys, (as_, 0), (ps, zblk), wo, OUT_TM, fg)
    out = [yp.reshape(bp, lp, dm), ys.reshape(bs, ls, dm)]
    for nm in names:
        out.append(jnp.stack(new_p[nm]))
        out.append(jnp.stack(new_s[nm]))
    return tuple(out)
```

```python
import functools
import math

import jax
import jax.numpy as jnp
from jax import lax
from jax.experimental import pallas as pl
from jax.experimental.pallas import tpu as pltpu

F32 = jnp.float32
BF16 = jnp.bfloat16
RMS_EPS = 1e-6
HEAD_NORM_EPS = 1e-6
LANES = 128
SUBLANES = 8
VMEM_LIMIT = 48 * 1024 * 1024
PAST_LEN = 8192
N_MIXERS = 4
PROJ_TM, PROJ_TN = 1024, 1024
OUT_TM = 512
GLU_TM, GLU_TN = 512, 1024


def _cparams(*sem):
    return pltpu.CompilerParams(dimension_semantics=sem, vmem_limit_bytes=VMEM_LIMIT)


def _sigmoid(x):
    return 1.0 / (1.0 + jnp.exp(-x))


def _dot_nt(a, b):
    return lax.dot_general(a, b, (((1,), (1,)), ((), ())), preferred_element_type=F32)


def _norm_matmul_body(x_ref, g_ref, w_ref, o_ref, h_scr):
    @pl.when(pl.program_id(1) == 0)
    def _():
        xf = x_ref[...]
        ms = jnp.mean(xf * xf, axis=-1, keepdims=True)
        h_scr[...] = (xf * lax.rsqrt(ms + RMS_EPS) * g_ref[...]).astype(BF16)

    o_ref[...] = jnp.dot(h_scr[...], w_ref[...], preferred_element_type=F32)


def _norm_matmul(x2d, g, w_bf16, tm, tn):
    m, k = x2d.shape
    n = w_bf16.shape[1]
    tm, tn = min(tm, m), min(tn, n)
    assert m % tm == 0 and n % tn == 0
    return pl.pallas_call(
        _norm_matmul_body,
        grid=(m // tm, n // tn),
        in_specs=[pl.BlockSpec((tm, k), lambda i, j: (i, 0)),
                  pl.BlockSpec((1, k), lambda i, j: (0, 0)),
                  pl.BlockSpec((k, tn), lambda i, j: (0, j))],
        out_specs=pl.BlockSpec((tm, tn), lambda i, j: (i, j)),
        out_shape=jax.ShapeDtypeStruct((m, n), F32),
        scratch_shapes=[pltpu.VMEM((tm, k), BF16)],
        compiler_params=_cparams("arbitrary", "arbitrary"),
        name="norm_matmul",
    )(x2d, g.reshape(1, k), w_bf16)


def _gated_out_body(*refs, final):
    if final:
        x_ref, a_ref, z_ref, w_ref, fg_ref, o_ref = refs
    else:
        x_ref, a_ref, z_ref, w_ref, o_ref = refs
    z = z_ref[...]
    act = (a_ref[...] * (z * _sigmoid(z))).astype(BF16)
    y = x_ref[...] + jnp.dot(act, w_ref[...], preferred_element_type=F32)
    if final:
        ms = jnp.mean(y * y, axis=-1, keepdims=True)
        y = y * lax.rsqrt(ms + RMS_EPS) * fg_ref[...]
    o_ref[...] = y


def _gated_out(x2d, a_src, z_src, w_bf16, tm, final_g=None):
    m, d = x2d.shape
    e = w_bf16.shape[0]
    tm = min(tm, m)
    assert m % tm == 0
    (a_arr, a_blk), (z_arr, z_blk) = a_src, z_src
    in_specs = [pl.BlockSpec((tm, d), lambda i: (i, 0)),
                pl.BlockSpec((tm, e), lambda i: (i, a_blk)),
                pl.BlockSpec((tm, e), lambda i: (i, z_blk)),
                pl.BlockSpec((e, d), lambda i: (0, 0))]
    args = [x2d, a_arr, z_arr, w_bf16]
    if final_g is not None:
        in_specs.append(pl.BlockSpec((1, d), lambda i: (0, 0)))
        args.append(final_g.reshape(1, d))
    return pl.pallas_call(
        functools.partial(_gated_out_body, final=final_g is not None),
        grid=(m // tm,),
        in_specs=in_specs,
        out_specs=pl.BlockSpec((tm, d), lambda i: (i, 0)),
        out_shape=jax.ShapeDtypeStruct((m, d), F32),
        compiler_params=_cparams("arbitrary"),
        name="gated_out",
    )(*args)


MLSTM_CHUNK = 256
MLSTM_MIN_CHUNK = 128
CONV_HALO = SUBLANES


def _log_sigmoid(x):
    return jnp.minimum(x, 0.0) - jnp.log1p(jnp.exp(-jnp.abs(x)))


def _mlstm_body(xm_ref, op_ref, gt_ref, bg_ref, cw_ref, cb_ref, wq_ref, wk_ref, wv_ref, ng_ref, sk_ref,
                cp_ref, c0_ref, n0_ref, m0_ref,
                hn_ref, cs_ref, c_ref, n_ref, m_ref, ext, *, t_in, t, heads):
    h, c = pl.program_id(0), pl.program_id(2)
    kw = cw_ref.shape[0]
    dh = xm_ref.shape[1]
    lo = CONV_HALO - (kw - 1)

    @pl.when(c == 0)
    def _():
        c_ref[...] = c0_ref[...]
        n_ref[...] = n0_ref[...]
        m_ref[...] = m0_ref[...]
        ext[lo:CONV_HALO, :] = cp_ref[0]
        if t_in < t:
            ext[CONV_HALO + t_in:CONV_HALO + t, :] = jnp.zeros((t - t_in, dh), F32)

    @pl.when(c > 0)
    def _():
        ext[lo:CONV_HALO, :] = ext[lo + t_in:CONV_HALO + t_in, :]

    ext[CONV_HALO:CONV_HALO + t_in, :] = xm_ref[...]
    cs_ref[0] = ext[lo + t_in:CONV_HALO + t_in, :]

    xm = ext[CONV_HALO:CONV_HALO + t, :]
    xconv = cb_ref[...] + ext[lo:lo + t, :] * cw_ref[0:1, :]
    for i in range(1, kw):
        xconv = xconv + ext[lo + i:lo + i + t, :] * cw_ref[i:i + 1, :]
    xc = xconv * _sigmoid(xconv)
    xcb = xc.astype(BF16)
    q = jnp.dot(xcb, wq_ref[0], preferred_element_type=F32)
    k = jnp.dot(xcb, wk_ref[0], preferred_element_type=F32) * (dh ** -0.5)
    v = jnp.dot(xm.astype(BF16), wv_ref[0], preferred_element_type=F32)

    gt = gt_ref[...] + bg_ref[...]
    row = lax.broadcasted_iota(jnp.int32, (t, 1), 0)
    if t_in < t:
        gt = jnp.concatenate([gt, jnp.zeros((t - t_in, gt.shape[1]), F32)], axis=0)
        valid = row < t_in
        ig_all = jnp.where(valid, gt, -jnp.inf)
        f_all = jnp.where(valid, _log_sigmoid(gt), 0.0)
    else:
        ig_all = gt
        f_all = _log_sigmoid(gt)
    sh = 1
    while sh < t:
        f_all = f_all + jnp.where(row >= sh, pltpu.roll(f_all, sh, 0), 0.0)
        sh *= 2
    lane = lax.broadcasted_iota(jnp.int32, (1, gt.shape[1]), 1)
    sub = lax.broadcasted_iota(jnp.int32, (gt.shape[1], 1), 0)
    col_of = lambda x, idx: jnp.sum(jnp.where(lane == idx, x, 0.0), axis=1, keepdims=True)
    row_of = lambda xt, idx: jnp.sum(jnp.where(sub == idx, xt, 0.0), axis=0, keepdims=True)
    f_col, ig_col = col_of(f_all, heads + h), col_of(ig_all, h)
    f_row, ig_row = row_of(f_all.T, heads + h), row_of(ig_all.T, h)

    m = m_ref[0]
    colidx = lax.broadcasted_iota(jnp.int32, (1, t), 1)
    dlog = jnp.where(row >= colidx, f_col - f_row + ig_row, -jnp.inf)
    inter = f_col + m
    mt = jnp.maximum(inter, jnp.max(dlog, axis=1, keepdims=True))
    w = jnp.exp(dlog - mt)
    a = jnp.exp(inter - mt)
    qb, kb, vb = q.astype(BF16), k.astype(BF16), v.astype(BF16)
    s = _dot_nt(qb, kb) * w
    cmat = c_ref[0]
    num = a * jnp.dot(qb, cmat.astype(BF16), preferred_element_type=F32) \
        + jnp.dot(s.astype(BF16), vb, preferred_element_type=F32)
    nvec = n_ref[0]
    den = a * jnp.sum(q * nvec, axis=1, keepdims=True) + jnp.sum(s, axis=1, keepdims=True)
    hloc = num / jnp.maximum(jnp.abs(den), jnp.exp(-mt))
    m_new = mt[t - 1:t, :]
    f_tot = f_col[t - 1:t, :]
    decay = jnp.exp(f_tot + m - m_new)
    ws = jnp.exp(f_tot - f_col + ig_col - m_new)
    kws = k * ws
    c_ref[0] = decay * cmat + lax.dot_general(kws.astype(BF16), vb, (((0,), (0,)), ((), ())),
                                              preferred_element_type=F32)
    n_ref[0] = decay * nvec + jnp.sum(kws, axis=0, keepdims=True)
    m_ref[0] = m_new

    o = hloc[:t_in, :] * _sigmoid(op_ref[...])
    mu = jnp.mean(o, axis=1, keepdims=True)
    var = jnp.mean(jnp.square(o - mu), axis=1, keepdims=True)
    hn = (o - mu) * lax.rsqrt(var + HEAD_NORM_EPS)
    hn_ref[...] = hn * ng_ref[...] + sk_ref[...] * xc[:t_in, :]


def _mlstm_pallas(proj2d, gates2d, bsz, seq, t_in, t, conv_prev, c0, n0, m0, layer, b_gate, conv_w, conv_b,
                  w_q, w_k, w_v, norm_g, skip):
    heads, dh, _ = w_q.shape
    e = heads * dh
    kw = conv_w.shape[0]
    gl = gates2d.shape[1]
    nl = c0.shape[0]
    assert seq % t_in == 0 and t_in <= t and t_in % SUBLANES == 0 and kw - 1 <= min(CONV_HALO, t_in)
    nc = seq // t_in
    rows = lambda blk: pl.BlockSpec((t_in, dh), lambda h, b, c: (b * nc + c, blk(h)))
    per_head_vec = lambda r: pl.BlockSpec((r, dh), lambda h, b, c: (0, h))
    wspec = pl.BlockSpec((1, dh, dh), lambda h, b, c: (h, 0, 0))
    st = lambda shp, off: pl.BlockSpec((1,) + shp, lambda h, b, c: (off + b * heads + h, 0, 0))
    cpspec = lambda off: pl.BlockSpec((1, kw - 1, dh), lambda h, b, c: (off + b, 0, h))
    sbase = layer * bsz * heads
    hn, cs, c_new, n_new, m_new = pl.pallas_call(
        functools.partial(_mlstm_body, t_in=t_in, t=t, heads=heads),
        grid=(heads, bsz, nc),
        in_specs=[rows(lambda h: h), rows(lambda h: 2 * heads + h),
                  pl.BlockSpec((t_in, gl), lambda h, b, c: (b * nc + c, 0)),
                  pl.BlockSpec((1, gl), lambda h, b, c: (0, 0)),
                  per_head_vec(kw), per_head_vec(1), wspec, wspec, wspec, per_head_vec(1), per_head_vec(1),
                  cpspec(layer * bsz), st((dh, dh), sbase), st((1, dh), sbase), st((1, 1), sbase)],
        out_specs=[rows(lambda h: h), cpspec(0), st((dh, dh), 0), st((1, dh), 0), st((1, 1), 0)],
        out_shape=[jax.ShapeDtypeStruct((bsz * seq, e), F32), jax.ShapeDtypeStruct((bsz, kw - 1, e), F32),
                   jax.ShapeDtypeStruct((bsz * heads, dh, dh), F32), jax.ShapeDtypeStruct((bsz * heads, 1, dh), F32),
                   jax.ShapeDtypeStruct((bsz * heads, 1, 1), F32)],
        scratch_shapes=[pltpu.VMEM((CONV_HALO + t, dh), F32)],
        compiler_params=_cparams("arbitrary", "arbitrary", "arbitrary"),
        name="mlstm",
    )(proj2d, proj2d, gates2d, jnp.pad(b_gate, (0, gl - b_gate.shape[0])).reshape(1, gl),
      conv_w, conv_b.reshape(1, e), w_q.astype(BF16), w_k.astype(BF16), w_v.astype(BF16),
      norm_g.reshape(1, e), skip.reshape(1, e), conv_prev.reshape(nl * bsz, kw - 1, e),
      c0.reshape(nl * bsz * heads, dh, dh), n0.reshape(nl * bsz * heads, 1, dh), m0.reshape(nl * bsz * heads, 1, 1))
    return (hn, cs, c_new.reshape(bsz, heads, dh, dh), n_new.reshape(bsz, heads, dh), m_new.reshape(bsz, heads))


def _s5_disc_body(lr_ref, li_ref, ldt_ref, br_ref, bi_ref, ar_ref, ai_ref, bbr_ref, bbi_ref):
    lr = jnp.minimum(lr_ref[...], -1e-4)
    li = li_ref[...]
    dt = jnp.exp(ldt_ref[...])
    mag = jnp.exp(dt * lr)
    a_re, a_im = mag * jnp.cos(dt * li), mag * jnp.sin(dt * li)
    den = lr * lr + li * li
    xr, xi = a_re - 1.0, a_im
    cr = (xr * lr + xi * li) / den
    ci = (xi * lr - xr * li) / den
    ar_ref[...] = a_re
    ai_ref[...] = a_im
    b_r, b_i = br_ref[...], bi_ref[...]
    bbr_ref[...] = cr * b_r - ci * b_i
    bbi_ref[...] = cr * b_i + ci * b_r


def _s5_discretise_pallas(lam_re, lam_im, log_dt, b_re, b_im):
    g, p, c = b_re.shape
    vm = pl.BlockSpec(memory_space=pltpu.VMEM)
    a_re, a_im, bb_re, bb_im = pl.pallas_call(
        _s5_disc_body,
        in_specs=[vm] * 5,
        out_specs=[vm] * 4,
        out_shape=[jax.ShapeDtypeStruct((g, 1, p), F32)] * 2 + [jax.ShapeDtypeStruct((g, c, p), F32)] * 2,
        name="s5_discretise",
    )(lam_re.reshape(g, 1, p), lam_im.reshape(g, 1, p), log_dt.reshape(g, 1, 1),
      jnp.swapaxes(b_re, 1, 2), jnp.swapaxes(b_im, 1, 2))
    return a_re.reshape(g, p), a_im.reshape(g, p), bb_re, bb_im


S5_STRIP_GROUPS = 8


def _gelu_tanh(x):
    return x * (0.5 * (1.0 + jnp.tanh(math.sqrt(2.0 / math.pi) * (x + 0.044715 * (x * x * x)))))


S5_SEGS = SUBLANES
S5_SEG_CHUNK = 512
S5_UNROLL = 64


def _cmul(ar, ai, br, bi):
    return ar * br - ai * bi, ar * bi + ai * br


def _s5_seg_body(u_ref, wbr_ref, wbi_ref, wcr_ref, wci_ref, ar_ref, ai_ref, d_ref, h0r_ref, h0i_ref,
                 g_ref, hr_ref, hi_ref, xr_scr, xi_scr, pwr_scr, pwi_scr, apr_scr, api_scr, *, chained):
    s = pl.program_id(2)
    nseg = S5_SEGS
    rows = u_ref.shape[0]
    jn = rows // nseg
    ar, ai = ar_ref[s], ai_ref[s]

    @pl.when((pl.program_id(0) == 0) & (pl.program_id(1) == 0))
    def _():
        pr, pi = ar, ai
        for j in range(jn):
            pwr_scr[s, j:j + 1, :] = pr
            pwi_scr[s, j:j + 1, :] = pi
            if j + 1 < jn:
                pr, pi = _cmul(pr, pi, ar, ai)
        qr, qi = jnp.ones_like(ar), jnp.zeros_like(ai)
        for k in range(nseg + 1):
            apr_scr[s, k:k + 1, :] = qr
            api_scr[s, k:k + 1, :] = qi
            qr, qi = _cmul(qr, qi, pr, pi)

    if chained:
        @pl.when(pl.program_id(1) == 0)
        def _():
            hr_ref[s] = h0r_ref[s]
            hi_ref[s] = h0i_ref[s]

    up = jnp.concatenate([u_ref[pl.ds(j, nseg, stride=jn), :] for j in range(jn)], axis=0)
    ub = up.astype(BF16)
    xr_scr[...] = jnp.dot(ub, wbr_ref[s], preferred_element_type=F32)
    xi_scr[...] = jnp.dot(ub, wbi_ref[s], preferred_element_type=F32)
    blk_of = lambda j: pl.ds(pl.multiple_of(j * nseg, nseg), nseg)

    def scan_step(j, h):
        tr, ti = _cmul(ar, ai, h[0], h[1])
        hr, hi = tr + xr_scr[blk_of(j), :], ti + xi_scr[blk_of(j), :]
        xr_scr[blk_of(j), :] = hr
        xi_scr[blk_of(j), :] = hi
        return hr, hi

    zero = jnp.zeros((nseg, ar.shape[1]), F32)
    hr, hi = lax.fori_loop(0, jn, scan_step, (zero, zero), unroll=min(jn, S5_UNROLL))
    anr, ani = pwr_scr[s, jn - 1:jn, :], pwi_scr[s, jn - 1:jn, :]
    if chained:
        sub = lax.broadcasted_iota(jnp.int32, (nseg, 1), 0)
        er, ei, pr, pi = hr, hi, anr, ani
        sh = 1
        while sh < nseg:
            keep = sub >= sh
            tr, ti = _cmul(pr, pi, jnp.where(keep, pltpu.roll(er, sh, 0), 0.0),
                           jnp.where(keep, pltpu.roll(ei, sh, 0), 0.0))
            er, ei = er + tr, ei + ti
            pr, pi = _cmul(pr, pi, pr, pi)
            sh *= 2
        hin_r, hin_i = hr_ref[s], hi_ref[s]
        tr, ti = _cmul(apr_scr[s, 0:nseg, :], api_scr[s, 0:nseg, :], hin_r, hin_i)
        cr = jnp.where(sub >= 1, pltpu.roll(er, 1, 0), 0.0) + tr
        ci = jnp.where(sub >= 1, pltpu.roll(ei, 1, 0), 0.0) + ti
        tr, ti = _cmul(apr_scr[s, 1:nseg + 1, :], api_scr[s, 1:nseg + 1, :], hin_r, hin_i)
        hr_ref[s] = (er + tr)[nseg - 1:nseg, :]
        hi_ref[s] = (ei + ti)[nseg - 1:nseg, :]
    else:
        cr, ci = h0r_ref[...], h0i_ref[...]
        tr, ti = _cmul(anr, ani, cr, ci)
        hr_ref[...] = hr + tr
        hi_ref[...] = hi + ti
    def fix_step(j, carry):
        tr, ti = _cmul(pwr_scr[s, pl.ds(j, 1), :], pwi_scr[s, pl.ds(j, 1), :], cr, ci)
        xr_scr[blk_of(j), :] = xr_scr[blk_of(j), :] + tr
        xi_scr[blk_of(j), :] = xi_scr[blk_of(j), :] + ti
        return carry

    lax.fori_loop(0, jn, fix_step, 0, unroll=min(jn, S5_UNROLL))
    y = (jnp.dot(xr_scr[...].astype(BF16), wcr_ref[s], preferred_element_type=F32)
         - jnp.dot(xi_scr[...].astype(BF16), wci_ref[s], preferred_element_type=F32))
    g = _gelu_tanh(y + d_ref[...] * up)
    for j in range(jn):
        g_ref[pl.ds(j, nseg, stride=jn), :] = g[j * nseg:(j + 1) * nseg, :]


def _s5_seg_pallas(proj2d, bsz, seq, chunk, h0_re, h0_im, a_re, a_im, bb_re, bb_im, c_re, c_im, d_skip):
    g, cch, p = bb_re.shape
    e = g * cch
    sg = S5_STRIP_GROUPS
    ns = g // sg
    lu, ls = sg * cch, sg * p
    assert g % sg == 0 and lu == LANES
    chained = seq % chunk == 0
    if chained:
        rows, n0, nc = chunk, bsz, seq // chunk
    else:
        assert bsz % S5_SEGS == 0
        rows, n0, nc = S5_SEGS * seq, bsz // S5_SEGS, 1
    jn = rows // S5_SEGS
    eye = jnp.eye(sg, dtype=F32)
    wb = lambda bb: jnp.einsum('sgcp,gh->sgchp', bb.reshape(ns, sg, cch, p), eye).reshape(ns, lu, ls).astype(BF16)
    wc = lambda cc: jnp.einsum('sgcp,gh->sgphc', cc.reshape(ns, sg, cch, p), eye).reshape(ns, ls, lu).astype(BF16)
    const3 = lambda shp: pl.BlockSpec(shp, lambda b, c, s: (0, 0, 0))
    if chained:
        hspec = pl.BlockSpec((ns, 1, ls), lambda b, c, s: (b, 0, 0))
        hshape = (bsz * ns, 1, ls)
    else:
        hspec = pl.BlockSpec((S5_SEGS, ls), lambda b, c, s: (b, s))
        hshape = (bsz, ns * ls)
    uspec = pl.BlockSpec((rows, lu), lambda b, c, s: (b * nc + c, s))
    gout, hr, hi = pl.pallas_call(
        functools.partial(_s5_seg_body, chained=chained),
        grid=(n0, nc, ns),
        in_specs=[uspec, const3((ns, lu, ls)), const3((ns, lu, ls)), const3((ns, ls, lu)), const3((ns, ls, lu)),
                  const3((ns, 1, ls)), const3((ns, 1, ls)),
                  pl.BlockSpec((1, lu), lambda b, c, s: (0, s)), hspec, hspec],
        out_specs=[uspec, hspec, hspec],
        out_shape=[jax.ShapeDtypeStruct((bsz * seq, e), F32),
                   jax.ShapeDtypeStruct(hshape, F32), jax.ShapeDtypeStruct(hshape, F32)],
        scratch_shapes=[pltpu.VMEM((rows, ls), F32), pltpu.VMEM((rows, ls), F32),
                        pltpu.VMEM((ns, jn, ls), F32), pltpu.VMEM((ns, jn, ls), F32),
                        pltpu.VMEM((ns, 2 * S5_SEGS, ls), F32), pltpu.VMEM((ns, 2 * S5_SEGS, ls), F32)],
        compiler_params=_cparams("arbitrary", "arbitrary", "arbitrary"),
        name="s5_scan",
    )(proj2d, wb(bb_re), wb(bb_im), wc(c_re), wc(c_im),
      a_re.reshape(ns, 1, ls), a_im.reshape(ns, 1, ls), d_skip.reshape(1, e),
      h0_re.reshape(hshape), h0_im.reshape(hshape))
    return gout, hr.reshape(bsz, g, p), hi.reshape(bsz, g, p)


def _glu_body(g_ref, gc_ref, w_ref, o_ref, gb_scr):
    @pl.when(pl.program_id(1) == 0)
    def _():
        gb_scr[...] = g_ref[...].astype(BF16)

    o_ref[...] = gc_ref[...] * _sigmoid(jnp.dot(gb_scr[...], w_ref[...], preferred_element_type=F32))


def _glu(g2d, w_bf16, tm, tn):
    m, e = g2d.shape
    tm, tn = min(tm, m), min(tn, e)
    assert m % tm == 0 and e % tn == 0
    return pl.pallas_call(
        _glu_body,
        grid=(m // tm, e // tn),
        in_specs=[pl.BlockSpec((tm, e), lambda i, j: (i, 0)),
                  pl.BlockSpec((tm, tn), lambda i, j: (i, j)),
                  pl.BlockSpec((e, tn), lambda i, j: (0, j))],
        out_specs=pl.BlockSpec((tm, tn), lambda i, j: (i, j)),
        out_shape=jax.ShapeDtypeStruct((m, e), F32),
        scratch_shapes=[pltpu.VMEM((tm, e), BF16)],
        compiler_params=_cparams("arbitrary", "arbitrary"),
        name="glu",
    )(g2d, g2d, w_bf16)


POOL_WINDOWS = (2, 4, 8, 16)
POOL_HALO = 2 * SUBLANES
POOL_CHUNK = 512


def _pool_body(u_ref, pre_ref, w_ref, sc_ref, o_ref, st_ref, ext, *, windows, start):
    c = pl.program_id(1)
    t = u_ref.shape[0]
    dp = max(windows) - 1
    dg = u_ref.shape[1] // len(windows)

    @pl.when(c == 0)
    def _():
        ext[POOL_HALO - dp:POOL_HALO, :] = pre_ref[0]

    @pl.when(c > 0)
    def _():
        ext[POOL_HALO - dp:POOL_HALO, :] = ext[POOL_HALO + t - dp:POOL_HALO + t, :]

    ext[POOL_HALO:POOL_HALO + t, :] = u_ref[...]
    st_ref[0] = ext[POOL_HALO + t - dp:POOL_HALO + t, :]
    pos = start + c * t + lax.broadcasted_iota(jnp.int32, (t, 1), 0)
    for g, w in enumerate(windows):
        lo = g * dg
        cur = ext[POOL_HALO:POOL_HALO + t, lo:lo + dg]
        tot = cur
        for j in range(1, w):
            tot = tot + ext[POOL_HALO - j:POOL_HALO - j + t, lo:lo + dg]
        cnt = jnp.minimum(pos + 1, w).astype(F32)
        mix = (tot / cnt - cur).astype(BF16)
        o_ref[:, lo:lo + dg] = jnp.dot(mix, w_ref[g], preferred_element_type=F32) * sc_ref[:, lo:lo + dg]


def _pool_pallas(proj2d, bsz, seq, chunk, prefix, start, w_grp, scale, windows):
    nw, dg, _ = w_grp.shape
    e = nw * dg
    dp = max(windows) - 1
    assert seq % chunk == 0 and prefix.shape == (bsz, dp, e) and dp < POOL_HALO and nw == len(windows)
    nc = seq // chunk
    return pl.pallas_call(
        functools.partial(_pool_body, windows=windows, start=start),
        grid=(bsz, nc),
        in_specs=[pl.BlockSpec((chunk, e), lambda b, c: (b * nc + c, 0)),
                  pl.BlockSpec((1, dp, e), lambda b, c: (b, 0, 0)),
                  pl.BlockSpec((nw, dg, dg), lambda b, c: (0, 0, 0)),
                  pl.BlockSpec((1, e), lambda b, c: (0, 0))],
        out_specs=[pl.BlockSpec((chunk, e), lambda b, c: (b * nc + c, 0)),
                   pl.BlockSpec((1, dp, e), lambda b, c: (b, 0, 0))],
        out_shape=[jax.ShapeDtypeStruct((bsz * seq, e), F32), jax.ShapeDtypeStruct((bsz, dp, e), F32)],
        scratch_shapes=[pltpu.VMEM((POOL_HALO + chunk, e), F32)],
        compiler_params=_cparams("arbitrary", "arbitrary"),
        name="pool",
    )(proj2d, prefix, w_grp.astype(BF16), scale.reshape(1, e))


DIL_PATTERNS = ((128, 1), (512, 4), (2048, 16))
DIL_BLOCK = 128
DIL_TQ = 2048
DIL_UNROLL = 8
DIL_MERGE_ROWS = 256


def _dil_prompt_body(*refs, dils, dk, nvp):
    q0_ref, q1_ref, q2_ref, k0_ref, k1_ref, k2_ref = refs[:6]
    v_refs = refs[6:6 + nvp]
    o_ref = refs[6 + nvp]
    per_pat = nvp + 2
    scr = refs[7 + nvp:]
    pat_scr = [scr[g * per_pat:(g + 1) * per_pat] for g in range(len(dils))]
    h, i = pl.program_id(1), pl.program_id(2)
    tq = o_ref.shape[0]
    qb = DIL_BLOCK
    nsub = tq // qb
    lane = lax.broadcasted_iota(jnp.int32, (1, LANES), 1)
    mine = (lane // dk) == (h % (LANES // dk))
    rowi = lax.broadcasted_iota(jnp.int32, (qb, 1), 0)
    coli = lax.broadcasted_iota(jnp.int32, (1, qb), 1)
    scale = dk ** -0.5
    for g, d in enumerate(dils):
        q_ref, k_ref = (q0_ref, q1_ref, q2_ref)[g], (k0_ref, k1_ref, k2_ref)[g]
        ld = d.bit_length() - 1

        acc, m_scr, l_scr = pat_scr[g][:nvp], pat_scr[g][nvp], pat_scr[g][nvp + 1]

        def body(idx, carry, q_ref=q_ref, k_ref=k_ref, d=d, ld=ld, acc=acc, m_scr=m_scr, l_scr=l_scr):
            r, bl = idx & (d - 1), idx >> ld
            lstart = r + (d * qb) * bl
            gstart = i * tq + lstart
            has_prev = gstart >= d * qb
            pstart = jnp.where(has_prev, gstart - d * qb, gstart)
            rows = lambda s: pl.ds(s, qb, stride=d) if d > 1 else pl.ds(s, qb)
            qm = (jnp.where(mine, q_ref[rows(lstart), :], 0.0) * scale).astype(BF16)
            sc = _dot_nt(qm, k_ref[rows(gstart), :].astype(BF16))
            sp = _dot_nt(qm, k_ref[rows(pstart), :].astype(BF16))
            sc = jnp.where(coli <= rowi, sc, -jnp.inf)
            sp = jnp.where((coli >= rowi) & has_prev, sp, -jnp.inf)
            mb = jnp.max(jnp.maximum(sc, sp), axis=1, keepdims=True)
            pc, pp = jnp.exp(sc - mb), jnp.exp(sp - mb)
            lb = jnp.sum(pc + pp, axis=1, keepdims=True)
            vrows = lambda s: jnp.concatenate([v[rows(s), :] for v in v_refs], axis=1).astype(BF16)
            nb = (jnp.dot(pc.astype(BF16), vrows(gstart), preferred_element_type=F32)
                  + jnp.dot(pp.astype(BF16), vrows(pstart), preferred_element_type=F32))
            for p, a in enumerate(acc):
                a[rows(lstart), :] = nb[:, p * LANES:(p + 1) * LANES]
            m_scr[rows(lstart), :] = jnp.broadcast_to(mb, (qb, LANES))
            l_scr[rows(lstart), :] = jnp.broadcast_to(lb, (qb, LANES))
            return carry

        lax.fori_loop(0, nsub, body, 0, unroll=DIL_UNROLL)

    def merge(c, carry):
        rs = pl.ds(pl.multiple_of(c * DIL_MERGE_ROWS, DIL_MERGE_ROWS), DIL_MERGE_ROWS)
        ms = [ps[nvp][rs, :] for ps in pat_scr]
        mmax = functools.reduce(jnp.maximum, ms)
        es = [jnp.exp(m - mmax) for m in ms]
        den = sum(ps[nvp + 1][rs, :] * e for ps, e in zip(pat_scr, es))
        for p in range(nvp):
            o_ref[rs, p * LANES:(p + 1) * LANES] = sum(ps[p][rs, :] * e for ps, e in zip(pat_scr, es)) / den
        return carry

    lax.fori_loop(0, tq // DIL_MERGE_ROWS, merge, 0)


def _dil_prompt_pallas(proj2d, bsz, seq, heads, dk, dv, col_q, col_k, col_v):
    dils = tuple(d for _, d in DIL_PATTERNS)
    assert all(w == d * DIL_BLOCK for w, d in DIL_PATTERNS)
    tq = min(DIL_TQ, seq)
    assert seq % tq == 0 and tq % (DIL_BLOCK * max(dils)) == 0 and LANES % dk == 0 and dv % LANES == 0
    nq = seq // tq
    hpb = LANES // dk
    nqk = heads * dk
    nvp = dv // LANES
    qspec = lambda g: pl.BlockSpec((tq, LANES), lambda b, h, i: (b * nq + i, (col_q + g * nqk) // LANES + h // hpb))
    kspec = lambda g: pl.BlockSpec((seq, LANES), lambda b, h, i: (b, (col_k + g * nqk) // LANES + h // hpb))
    vspec = lambda p: pl.BlockSpec((seq, LANES), lambda b, h, i: (b, col_v // LANES + h * nvp + p))
    return pl.pallas_call(
        functools.partial(_dil_prompt_body, dils=dils, dk=dk, nvp=nvp),
        grid=(bsz, heads, nq),
        in_specs=[qspec(0), qspec(1), qspec(2), kspec(0), kspec(1), kspec(2)] + [vspec(p) for p in range(nvp)],
        out_specs=pl.BlockSpec((tq, dv), lambda b, h, i: (b * nq + i, h)),
        out_shape=jax.ShapeDtypeStruct((bsz * seq, heads * dv), F32),
        scratch_shapes=[pltpu.VMEM((tq, LANES), F32)] * ((2 + nvp) * len(dils)),
        compiler_params=_cparams("arbitrary", "arbitrary", "arbitrary"),
        name="dilated_prompt",
    )(*([proj2d] * (6 + nvp)))


def _dil_sample_body(q0_ref, q1_ref, q2_ref, n0_ref, n1_ref, n2_ref, vn_ref, c0_ref, c1_ref, c2_ref, vc_ref,
                     o_ref, oc0_ref, oc1_ref, oc2_ref, ovc_ref, *, wins, dils, dk):
    ls = q0_ref.shape[0]
    dvb = vn_ref.shape[1]
    hpb = LANES // dk
    lv = vc_ref.shape[1]
    lane = lax.broadcasted_iota(jnp.int32, (1, LANES), 1)
    qi = lax.broadcasted_iota(jnp.int32, (hpb * ls, 1), 0) % ls
    scale = dk ** -0.5
    vnew = vn_ref[...]
    parts = []
    for g, (win, d) in enumerate(zip(wins, dils)):
        q_ref, n_ref, c_ref = (q0_ref, q1_ref, q2_ref)[g], (n0_ref, n1_ref, n2_ref)[g], (c0_ref, c1_ref, c2_ref)[g]
        lk = c_ref.shape[1]
        qt = q_ref[...] * scale
        qbd = jnp.concatenate([jnp.where((lane // dk) == hh, qt, 0.0) for hh in range(hpb)], axis=0).astype(BF16)
        knew = n_ref[...]
        s1 = _dot_nt(qbd, c_ref[0].astype(BF16))
        s2 = _dot_nt(qbd, knew.astype(BF16))
        back1 = lk + qi - lax.broadcasted_iota(jnp.int32, (1, lk), 1)
        back2 = qi - lax.broadcasted_iota(jnp.int32, (1, ls), 1)
        s1 = jnp.where((back1 <= win) & ((back1 & (d - 1)) == 0), s1, -jnp.inf)
        s2 = jnp.where((back2 >= 0) & ((back2 & (d - 1)) == 0), s2, -jnp.inf)
        m = jnp.maximum(jnp.max(s1, axis=1, keepdims=True), jnp.max(s2, axis=1, keepdims=True))
        p1, p2 = jnp.exp(s1 - m), jnp.exp(s2 - m)
        den = jnp.sum(p1, axis=1, keepdims=True) + jnp.sum(p2, axis=1, keepdims=True)
        num = (jnp.dot(p1.astype(BF16), vc_ref[0, lv - lk:lv, :].astype(BF16), preferred_element_type=F32)
               + jnp.dot(p2.astype(BF16), vnew.astype(BF16), preferred_element_type=F32))
        parts.append((num, m, den))
        oc_ref = (oc0_ref, oc1_ref, oc2_ref)[g]
        oc_ref[0, 0:lk - ls, :] = c_ref[0, ls:lk, :]
        oc_ref[0, lk - ls:lk, :] = knew
    mmax = functools.reduce(jnp.maximum, [m for _, m, _ in parts])
    num = sum(nu * jnp.exp(m - mmax) for nu, m, _ in parts)
    den = sum(de * jnp.exp(m - mmax) for _, m, de in parts)
    o = num / den
    dvh = dvb // hpb
    o_ref[...] = jnp.concatenate([o[hh * ls:(hh + 1) * ls, hh * dvh:(hh + 1) * dvh] for hh in range(hpb)], axis=1)
    ovc_ref[0, 0:lv - ls, :] = vc_ref[0, ls:lv, :]
    ovc_ref[0, lv - ls:lv, :] = vnew


def _dil_sample_pallas(proj2d, bsz, ls, k_caches, v_cache, layer, col_q, col_k, col_v):
    heads, dk = k_caches[0].shape[3], k_caches[0].shape[4]
    dv = v_cache.shape[4]
    nl = v_cache.shape[0]
    wins, dils = tuple(w for w, _ in DIL_PATTERNS), tuple(d for _, d in DIL_PATTERNS)
    hpb = LANES // dk
    nqk = heads * dk
    lks = tuple(kc.shape[2] for kc in k_caches)
    lv = v_cache.shape[2]
    assert heads % hpb == 0 and ls % SUBLANES == 0
    assert all(lk <= lv and lk % SUBLANES == 0 and lk > ls for lk in lks)
    dvb = hpb * dv
    base = layer * bsz
    qspec = lambda col, g: pl.BlockSpec((ls, LANES), lambda b, h: (b, (col + g * nqk) // LANES + h))
    cspec = lambda lk, off: pl.BlockSpec((1, lk, LANES), lambda b, h: (off + b, 0, h))
    vcspec = lambda off: pl.BlockSpec((1, lv, dvb), lambda b, h: (off + b, 0, h))
    kc2 = [kc.reshape(nl * bsz, lk, nqk) for kc, lk in zip(k_caches, lks)]
    vc2 = v_cache.reshape(nl * bsz, lv, heads * dv)
    outs = pl.pallas_call(
        functools.partial(_dil_sample_body, wins=wins, dils=dils, dk=dk),
        grid=(bsz, heads // hpb),
        in_specs=[qspec(col_q, 0), qspec(col_q, 1), qspec(col_q, 2), qspec(col_k, 0), qspec(col_k, 1), qspec(col_k, 2),
                  pl.BlockSpec((ls, dvb), lambda b, h: (b, col_v // dvb + h)),
                  cspec(lks[0], base), cspec(lks[1], base), cspec(lks[2], base), vcspec(base)],
        out_specs=[pl.BlockSpec((ls, dvb), lambda b, h: (b, h)),
                   cspec(lks[0], 0), cspec(lks[1], 0), cspec(lks[2], 0), vcspec(0)],
        out_shape=[jax.ShapeDtypeStruct((bsz * ls, heads * dv), F32)]
        + [jax.ShapeDtypeStruct((bsz, lk, nqk), F32) for lk in lks] + [jax.ShapeDtypeStruct((bsz, lv, heads * dv), F32)],
        compiler_params=_cparams("arbitrary", "arbitrary"),
        name="dilated_sample",
    )(*([proj2d] * 7), *kc2, vc2)
    o, nk0, nk1, nk2, nv = outs
    new_k = tuple(a.reshape(bsz, lk, heads, dk) for a, lk in zip((nk0, nk1, nk2), lks))
    return o, new_k, nv.reshape(bsz, lv, heads, dv)


DIL_S_ROWS = 512


def _dil_decode_body(q_ref, kn_ref, vn_ref, k1_ref, k2_ref, k3_ref, v_ref, k3x_ref, vx_ref,
                     o_ref, k1o_ref, k2o_ref, k3o_ref, vo_ref, m_scr, l_scr, acc_scr, *, dils, dk):
    c, nch = pl.program_id(1), pl.num_programs(1)
    ls = q_ref.shape[2]
    d2, d3 = dils[1], dils[2]
    rc = v_ref.shape[1] * v_ref.shape[2]
    scale = dk ** -0.5
    lanes = lambda x: jnp.broadcast_to(x, x.shape[:-1] + (LANES,))
    scores = lambda ks, q: jnp.sum(ks * q, axis=-1, keepdims=True)

    @pl.when(c == 0)
    def _():
        m_scr[...] = jnp.full(m_scr.shape, -jnp.inf, F32)
        l_scr[...] = jnp.zeros(l_scr.shape, F32)
        acc_scr[...] = jnp.zeros(acc_scr.shape, F32)

    for i in range(ls):
        q = q_ref[0, 2, i] * scale
        s = scores(k3_ref[0, :, i], q)
        m_old = m_scr[i][:, 0:1]
        m_new = jnp.maximum(m_old, jnp.max(s, axis=0))
        alpha = jnp.exp(m_old - m_new)
        e = jnp.exp(s - m_new)
        m_scr[i] = lanes(m_new)
        l_scr[i] = lanes(l_scr[i][:, 0:1] * alpha + jnp.sum(e, axis=0))
        acc_scr[i] = acc_scr[i] * alpha + jnp.sum(e * v_ref[0, :, i], axis=0)

    hk, hv = k3_ref.shape[3:], v_ref.shape[3:]
    k3o_ref[0, 0:rc - ls] = k3_ref[0].reshape((rc,) + hk)[ls:rc]
    vo_ref[0, 0:rc - ls] = v_ref[0].reshape((rc,) + hv)[ls:rc]

    @pl.when(c < nch - 1)
    def _():
        k3o_ref[0, rc - ls:rc] = k3x_ref[0, 0]
        vo_ref[0, rc - ls:rc] = vx_ref[0, 0]

    @pl.when(c == nch - 1)
    def _():
        k3o_ref[0, rc - ls:rc] = kn_ref[0, 2]
        vo_ref[0, rc - ls:rc] = vn_ref[0]
        l1, l2 = k1_ref.shape[1], k2_ref.shape[1] * d2
        k1o_ref[0, 0:l1 - ls] = k1_ref[0, ls:l1]
        k1o_ref[0, l1 - ls:l1] = kn_ref[0, 0]
        k2o_ref[0, 0:l2 - ls] = k2_ref[0].reshape((l2,) + hk)[ls:l2]
        k2o_ref[0, l2 - ls:l2] = kn_ref[0, 1]
        na = v_ref.shape[1]
        v2 = lambda r: jnp.stack([v_ref[0, na - l2 // d3:na, r + d2 * t] for t in range(d3 // d2)], axis=1) \
            .reshape((l2 // d2,) + hv)
        v1 = v_ref[0, na - l1 // d3:na].reshape((l1,) + hv)
        for i in range(ls):
            parts = []
            q = q_ref[0, 0, i] * scale
            ks = jnp.concatenate([k1_ref[0, i:l1], kn_ref[0, 0, 0:i + 1]], axis=0)
            vs = jnp.concatenate([v1[i:l1], vn_ref[0, 0:i + 1]], axis=0)
            parts.append((scores(ks, q), vs))
            q = q_ref[0, 1, i] * scale
            news = list(range(i % d2, i + 1, d2))
            ks = jnp.concatenate([k2_ref[0, i // d2:, i % d2]] + [kn_ref[0, 1, t:t + 1] for t in news], axis=0)
            vs = jnp.concatenate([v2(i % d2)[i // d2:]] + [vn_ref[0, t:t + 1] for t in news], axis=0)
            parts.append((scores(ks, q), vs))
            q = q_ref[0, 2, i] * scale
            s_new = scores(kn_ref[0, 2, i:i + 1], q)[0]
            m_old = m_scr[i][:, 0:1]
            m3 = jnp.maximum(m_old, s_new)
            alpha, e_new = jnp.exp(m_old - m3), jnp.exp(s_new - m3)
            den3 = l_scr[i][:, 0:1] * alpha + e_new
            num3 = acc_scr[i] * alpha + e_new * vn_ref[0, i]
            stats = []
            for s, vs in parts:
                m = jnp.max(s, axis=0)
                e = jnp.exp(s - m)
                stats.append((jnp.sum(e * vs, axis=0), m, jnp.sum(e, axis=0)))
            stats.append((num3, m3, den3))
            mmax = functools.reduce(jnp.maximum, [m for _, m, _ in stats])
            num = sum(nu * jnp.exp(m - mmax) for nu, m, _ in stats)
            den = sum(de * jnp.exp(m - mmax) for _, m, de in stats)
            o_ref[0, i] = num / den


def _dil_decode_pallas(proj2d, bsz, ls, k_caches, v_cache, layer, col_q, col_k, col_v):
    heads, dk = k_caches[0].shape[3], k_caches[0].shape[4]
    dv = v_cache.shape[4]
    nl = v_cache.shape[0]
    wins, dils = tuple(w for w, _ in DIL_PATTERNS), tuple(d for _, d in DIL_PATTERNS)
    l1, l2, l3 = (kc.shape[2] for kc in k_caches)
    lv = v_cache.shape[2]
    d1, d2, d3 = dils
    rc = DIL_S_ROWS
    assert (l1, l2, l3) == wins and lv == l3 and d1 == 1 and d3 % d2 == 0 and ls == SUBLANES and ls <= d3
    assert lv % rc == 0 and rc % d3 == 0 and l2 <= rc and l1 <= rc and l1 % d3 == 0 and l2 % d3 == 0
    nch = lv // rc
    npat = len(dils)
    nqk = heads * dk
    p3 = proj2d.reshape(bsz, ls, -1)
    pick = lambda col: jnp.swapaxes(p3[:, :, col:col + npat * nqk].reshape(bsz, ls, npat, heads, dk), 1, 2)
    q5, kn5 = pick(col_q), pick(col_k)
    vn4 = p3[:, :, col_v:col_v + heads * dv].reshape(bsz, ls, heads, dv)
    base = layer * bsz
    k1 = k_caches[0].reshape(nl * bsz, l1, heads, dk)
    k2 = k_caches[1].reshape(nl * bsz, l2 // d2, d2, heads, dk)
    k3 = k_caches[2].reshape(nl * bsz, l3 // d3, d3, heads, dk)
    v6 = v_cache.reshape(nl * bsz, lv // d3, d3, heads, dv)
    k3x = k_caches[2].reshape(nl * bsz, l3 // ls, ls, heads, dk)
    vx = v_cache.reshape(nl * bsz, lv // ls, ls, heads, dv)
    nxt = lambda b, c: (base + b, jnp.minimum((c + 1) * (rc // ls), lv // ls - 1), 0, 0, 0)
    bc4 = lambda shp: pl.BlockSpec((1,) + shp, lambda b, c: (b, 0, 0, 0))
    outs = pl.pallas_call(
        functools.partial(_dil_decode_body, dils=dils, dk=dk),
        grid=(bsz, nch),
        in_specs=[pl.BlockSpec((1, npat, ls, heads, dk), lambda b, c: (b, 0, 0, 0, 0)),
                  pl.BlockSpec((1, npat, ls, heads, dk), lambda b, c: (b, 0, 0, 0, 0)),
                  bc4((ls, heads, dv)),
                  pl.BlockSpec((1, l1, heads, dk), lambda b, c: (base + b, 0, 0, 0)),
                  pl.BlockSpec((1, l2 // d2, d2, heads, dk), lambda b, c: (base + b, 0, 0, 0, 0)),
                  pl.BlockSpec((1, rc // d3, d3, heads, dk), lambda b, c: (base + b, c, 0, 0, 0)),
                  pl.BlockSpec((1, rc // d3, d3, heads, dv), lambda b, c: (base + b, c, 0, 0, 0)),
                  pl.BlockSpec((1, 1, ls, heads, dk), nxt),
                  pl.BlockSpec((1, 1, ls, heads, dv), nxt)],
        out_specs=[bc4((ls, heads, dv)), bc4((l1, heads, dk)), bc4((l2, heads, dk)),
                   pl.BlockSpec((1, rc, heads, dk), lambda b, c: (b, c, 0, 0)),
                   pl.BlockSpec((1, rc, heads, dv), lambda b, c: (b, c, 0, 0))],
        out_shape=[jax.ShapeDtypeStruct((bsz, ls, heads, dv), F32), jax.ShapeDtypeStruct((bsz, l1, heads, dk), F32),
                   jax.ShapeDtypeStruct((bsz, l2, heads, dk), F32), jax.ShapeDtypeStruct((bsz, l3, heads, dk), F32),
                   jax.ShapeDtypeStruct((bsz, lv, heads, dv), F32)],
        scratch_shapes=[pltpu.VMEM((ls, heads, LANES), F32), pltpu.VMEM((ls, heads, LANES), F32),
                        pltpu.VMEM((ls, heads, dv), F32)],
        compiler_params=_cparams("arbitrary", "arbitrary"),
        name="dilated_decode",
    )(q5, kn5, vn4, k1, k2, k3, v6, k3x, vx)
    o, nk1, nk2, nk3, nv = outs
    return o.reshape(bsz * ls, heads * dv), (nk1, nk2, nk3), nv


def _mlstm_chunks(seq):
    if seq % MLSTM_CHUNK == 0:
        return MLSTM_CHUNK, MLSTM_CHUNK
    return seq, max(MLSTM_MIN_CHUNK, seq)


def _chunk_of(seq, chunk):
    return chunk if seq % chunk == 0 else seq


def kernel(x_prompt, x_sample, state_mlstm_c, state_mlstm_n, state_mlstm_m, state_mlstm_conv, state_s5_re, state_s5_im, cache_dil_k1, cache_dil_k2, cache_dil_k3, cache_dil_v, state_pool, norm_g, final_norm_g, a_w_in, a_b_gate, a_conv_w, a_conv_b, a_w_q, a_w_k, a_w_v, a_norm_g, a_skip, a_w_out, b_w_in, b_lam_re, b_lam_im, b_log_dt, b_B_re, b_B_im, b_C_re, b_C_im, b_d, b_w_glu, b_w_out, c_w_in, c_w_out, d_w_in, d_w_grp, d_scale, d_w_out):
    bp, lp, dm = x_prompt.shape
    bs, ls, _ = x_sample.shape
    depth = norm_g.shape[0]
    H, Dh = a_w_q.shape[1], a_w_q.shape[2]
    E = H * Dh
    heads, dk = cache_dil_k1.shape[3], cache_dil_k1.shape[4]
    dv = cache_dil_v.shape[4]
    nqk = heads * dk
    npat = len(DIL_PATTERNS)
    names = ('mlstm_c', 'mlstm_n', 'mlstm_m', 'mlstm_conv', 's5_re', 's5_im', 'k1', 'k2', 'k3', 'v', 'pool')
    new_p = {nm: [] for nm in names}
    new_s = {nm: [] for nm in names}
    yp = x_prompt.reshape(bp * lp, dm)
    ys = x_sample.reshape(bs * ls, dm)
    zeros = lambda *shape: jnp.zeros(shape, F32)

    def proj_both(layer, w):
        wb = w.astype(BF16)
        return (_norm_matmul(yp, norm_g[layer], wb, PROJ_TM, PROJ_TN),
                _norm_matmul(ys, norm_g[layer], wb, PROJ_TM, PROJ_TN))

    for layer in range(depth):
        kind, j = layer % N_MIXERS, layer // N_MIXERS
        fg = final_norm_g if layer == depth - 1 else None
        if kind == 0:
            pp, ps = proj_both(layer, a_w_in[j][:, :3 * E])
            gp, gs = proj_both(layer, jnp.pad(a_w_in[j][:, 3 * E:], ((0, 0), (0, LANES - 2 * H))))
            w = (a_b_gate[j], a_conv_w[j], a_conv_b[j], a_w_q[j], a_w_k[j], a_w_v[j], a_norm_g[j], a_skip[j])
            ap, *sp = _mlstm_pallas(pp, gp, bp, lp, *_mlstm_chunks(lp), zeros(1, bp, a_conv_w.shape[1] - 1, E),
                                    zeros(1, bp, H, Dh, Dh), zeros(1, bp, H, Dh), zeros(1, bp, H), 0, *w)
            as_, *ss = _mlstm_pallas(ps, gs, bs, ls, *_mlstm_chunks(ls), state_mlstm_conv, state_mlstm_c,
                                     state_mlstm_n, state_mlstm_m, j, *w)
            keys = ('mlstm_conv', 'mlstm_c', 'mlstm_n', 'mlstm_m')
            zblk, w_out = 1, a_w_out[j]
        elif kind == 1:
            pp, ps = proj_both(layer, b_w_in[j])
            a_re, a_im, bb_re, bb_im = _s5_discretise_pallas(b_lam_re[j], b_lam_im[j], b_log_dt[j], b_B_re[j], b_B_im[j])
            w = (a_re, a_im, bb_re, bb_im, b_C_re[j], b_C_im[j], b_d[j])
            zs = zeros(bp, *state_s5_re.shape[2:])
            gp_, *sp = _s5_seg_pallas(pp, bp, lp, S5_SEG_CHUNK, zs, zs, *w)
            gs_, *ss = _s5_seg_pallas(ps, bs, ls, S5_SEG_CHUNK, state_s5_re[j], state_s5_im[j], *w)
            wglu = b_w_glu[j].astype(BF16)
            ap, as_ = _glu(gp_, wglu, GLU_TM, GLU_TN), _glu(gs_, wglu, GLU_TM, GLU_TN)
            keys = ('s5_re', 's5_im')
            zblk, w_out = 1, b_w_out[j]
        elif kind == 2:
            wc = c_w_in[j]
            wc = jnp.concatenate([wc[:, 2 * npat * nqk + E:], wc[:, 2 * npat * nqk:2 * npat * nqk + E],
                                  wc[:, :2 * npat * nqk]], axis=1)
            col_v, col_q, col_k = E, 2 * E, 2 * E + npat * nqk
            pp, ps = proj_both(layer, wc)
            ap = _dil_prompt_pallas(pp, bp, lp, heads, dk, dv, col_q, col_k, col_v)
            pp3 = pp.reshape(bp, lp, -1)
            sp = [pp3[:, lp - min(win, lp):, col_k + g * nqk:col_k + (g + 1) * nqk].reshape(bp, -1, heads, dk)
                  for g, (win, _) in enumerate(DIL_PATTERNS)]
            sp.append(pp3[:, lp - min(DIL_PATTERNS[-1][0], lp):, col_v:col_v + E].reshape(bp, -1, heads, dv))
            as_, kq, vq = _dil_decode_pallas(ps, bs, ls, (cache_dil_k1, cache_dil_k2, cache_dil_k3), cache_dil_v, j,
                                             col_q, col_k, col_v)
            ss = (*kq, vq)
            keys = ('k1', 'k2', 'k3', 'v')
            zblk, w_out = 0, c_w_out[j]
        else:
            pp, ps = proj_both(layer, d_w_in[j])
            dp = max(POOL_WINDOWS) - 1
            ap, *sp = _pool_pallas(pp, bp, lp, _chunk_of(lp, POOL_CHUNK), zeros(bp, dp, E), 0, d_w_grp[j], d_scale[j],
                                   POOL_WINDOWS)
            as_, *ss = _pool_pallas(ps, bs, ls, _chunk_of(ls, POOL_CHUNK), state_pool[j], PAST_LEN, d_w_grp[j],
                                    d_scale[j], POOL_WINDOWS)
            keys = ('pool',)
            zblk, w_out = 1, d_w_out[j]
        for nm, a, b in zip(keys, sp, ss):
            new_p[nm].append(a)
            new_s[nm].append(b)
        wo = w_out.astype(BF16)
        yp = _gated_out(yp, (ap, 0), (pp, zblk), wo, OUT_TM, fg)
        ys = _gated_out(ys, (as_, 0), (ps, zblk), wo, OUT_TM, fg)
    out = [yp.reshape(bp, lp, dm), ys.reshape(bs, ls, dm)]
    for nm in names:
        out.append(jnp.stack(new_p[nm]))
        out.append(jnp.stack(new_s[nm]))
    return tuple(out)
```

```python
import functools
import math

import jax
import jax.numpy as jnp
from jax import lax
from jax.experimental import pallas as pl
from jax.experimental.pallas import tpu as pltpu

F32 = jnp.float32
BF16 = jnp.bfloat16
RMS_EPS = 1e-6
HEAD_NORM_EPS = 1e-6
LANES = 128
SUBLANES = 8
VMEM_LIMIT = 48 * 1024 * 1024
PAST_LEN = 8192
N_MIXERS = 4
PROJ_TM, PROJ_TN = 2048, 1024
OUT_TM = 512
GLU_TM = 512


def _cparams(*sem):
    return pltpu.CompilerParams(dimension_semantics=sem, vmem_limit_bytes=VMEM_LIMIT)


def _sigmoid(x):
    return 1.0 / (1.0 + jnp.exp(-x))


def _dot_nt(a, b):
    return lax.dot_general(a, b, (((1,), (1,)), ((), ())), preferred_element_type=F32)


def _norm_matmul_body(x_ref, g_ref, w_ref, o_ref, h_scr):
    @pl.when(pl.program_id(1) == 0)
    def _():
        xf = x_ref[...]
        ms = jnp.mean(xf * xf, axis=-1, keepdims=True)
        h_scr[...] = (xf * lax.rsqrt(ms + RMS_EPS) * g_ref[...]).astype(BF16)

    o_ref[...] = jnp.dot(h_scr[...], w_ref[...], preferred_element_type=F32)


def _norm_matmul(x2d, g, w_bf16, tm, tn):
    m, k = x2d.shape
    n = w_bf16.shape[1]
    tm, tn = min(tm, m), min(tn, n)
    assert m % tm == 0 and n % tn == 0
    return pl.pallas_call(
        _norm_matmul_body,
        grid=(m // tm, n // tn),
        in_specs=[pl.BlockSpec((tm, k), lambda i, j: (i, 0)),
                  pl.BlockSpec((1, k), lambda i, j: (0, 0)),
                  pl.BlockSpec((k, tn), lambda i, j: (0, j))],
        out_specs=pl.BlockSpec((tm, tn), lambda i, j: (i, j)),
        out_shape=jax.ShapeDtypeStruct((m, n), F32),
        scratch_shapes=[pltpu.VMEM((tm, k), BF16)],
        compiler_params=_cparams("arbitrary", "arbitrary"),
        name="norm_matmul",
    )(x2d, g.reshape(1, k), w_bf16)


def _gated_out_body(*refs, final):
    if final:
        x_ref, a_ref, z_ref, w_ref, fg_ref, o_ref = refs
    else:
        x_ref, a_ref, z_ref, w_ref, o_ref = refs
    z = z_ref[...]
    act = (a_ref[...] * (z * _sigmoid(z))).astype(BF16)
    y = x_ref[...] + jnp.dot(act, w_ref[...], preferred_element_type=F32)
    if final:
        ms = jnp.mean(y * y, axis=-1, keepdims=True)
        y = y * lax.rsqrt(ms + RMS_EPS) * fg_ref[...]
    o_ref[...] = y


def _gated_out(x2d, a_src, z_src, w_bf16, tm, final_g=None):
    m, d = x2d.shape
    e = w_bf16.shape[0]
    tm = min(tm, m)
    assert m % tm == 0
    (a_arr, a_blk), (z_arr, z_blk) = a_src, z_src
    in_specs = [pl.BlockSpec((tm, d), lambda i: (i, 0)),
                pl.BlockSpec((tm, e), lambda i: (i, a_blk)),
                pl.BlockSpec((tm, e), lambda i: (i, z_blk)),
                pl.BlockSpec((e, d), lambda i: (0, 0))]
    args = [x2d, a_arr, z_arr, w_bf16]
    if final_g is not None:
        in_specs.append(pl.BlockSpec((1, d), lambda i: (0, 0)))
        args.append(final_g.reshape(1, d))
    return pl.pallas_call(
        functools.partial(_gated_out_body, final=final_g is not None),
        grid=(m // tm,),
        in_specs=in_specs,
        out_specs=pl.BlockSpec((tm, d), lambda i: (i, 0)),
        out_shape=jax.ShapeDtypeStruct((m, d), F32),
        compiler_params=_cparams("arbitrary"),
        name="gated_out",
    )(*args)


MLSTM_CHUNK = 256
MLSTM_MIN_CHUNK = 128
CONV_HALO = SUBLANES


def _log_sigmoid(x):
    return jnp.minimum(x, 0.0) - jnp.log1p(jnp.exp(-jnp.abs(x)))


def _mlstm_body(xm_ref, op_ref, gt_ref, bg_ref, cw_ref, cb_ref, wq_ref, wk_ref, wv_ref, ng_ref, sk_ref,
                cp_ref, c0_ref, n0_ref, m0_ref,
                hn_ref, cs_ref, c_ref, n_ref, m_ref, ext, *, t_in, t, heads):
    h, c = pl.program_id(0), pl.program_id(2)
    kw = cw_ref.shape[0]
    dh = xm_ref.shape[1]
    lo = CONV_HALO - (kw - 1)

    @pl.when(c == 0)
    def _():
        c_ref[...] = c0_ref[...]
        n_ref[...] = n0_ref[...]
        m_ref[...] = m0_ref[...]
        ext[lo:CONV_HALO, :] = cp_ref[0]
        if t_in < t:
            ext[CONV_HALO + t_in:CONV_HALO + t, :] = jnp.zeros((t - t_in, dh), F32)

    @pl.when(c > 0)
    def _():
        ext[lo:CONV_HALO, :] = ext[lo + t_in:CONV_HALO + t_in, :]

    ext[CONV_HALO:CONV_HALO + t_in, :] = xm_ref[...]
    cs_ref[0] = ext[lo + t_in:CONV_HALO + t_in, :]

    xm = ext[CONV_HALO:CONV_HALO + t, :]
    xconv = cb_ref[...] + ext[lo:lo + t, :] * cw_ref[0:1, :]
    for i in range(1, kw):
        xconv = xconv + ext[lo + i:lo + i + t, :] * cw_ref[i:i + 1, :]
    xc = xconv * _sigmoid(xconv)
    xcb = xc.astype(BF16)
    q = jnp.dot(xcb, wq_ref[0], preferred_element_type=F32)
    k = jnp.dot(xcb, wk_ref[0], preferred_element_type=F32) * (dh ** -0.5)
    v = jnp.dot(xm.astype(BF16), wv_ref[0], preferred_element_type=F32)

    gt = gt_ref[...] + bg_ref[...]
    row = lax.broadcasted_iota(jnp.int32, (t, 1), 0)
    if t_in < t:
        gt = jnp.concatenate([gt, jnp.zeros((t - t_in, gt.shape[1]), F32)], axis=0)
        valid = row < t_in
        ig_all = jnp.where(valid, gt, -jnp.inf)
        f_all = jnp.where(valid, _log_sigmoid(gt), 0.0)
    else:
        ig_all = gt
        f_all = _log_sigmoid(gt)
    sh = 1
    while sh < t:
        f_all = f_all + jnp.where(row >= sh, pltpu.roll(f_all, sh, 0), 0.0)
        sh *= 2
    lane = lax.broadcasted_iota(jnp.int32, (1, gt.shape[1]), 1)
    sub = lax.broadcasted_iota(jnp.int32, (gt.shape[1], 1), 0)
    col_of = lambda x, idx: jnp.sum(jnp.where(lane == idx, x, 0.0), axis=1, keepdims=True)
    row_of = lambda xt, idx: jnp.sum(jnp.where(sub == idx, xt, 0.0), axis=0, keepdims=True)
    f_col, ig_col = col_of(f_all, heads + h), col_of(ig_all, h)
    f_row, ig_row = row_of(f_all.T, heads + h), row_of(ig_all.T, h)

    m = m_ref[0]
    colidx = lax.broadcasted_iota(jnp.int32, (1, t), 1)
    dlog = jnp.where(row >= colidx, f_col - f_row + ig_row, -jnp.inf)
    inter = f_col + m
    mt = jnp.maximum(inter, jnp.max(dlog, axis=1, keepdims=True))
    w = jnp.exp(dlog - mt)
    a = jnp.exp(inter - mt)
    qb, kb, vb = q.astype(BF16), k.astype(BF16), v.astype(BF16)
    s = _dot_nt(qb, kb) * w
    cmat = c_ref[0]
    num = a * jnp.dot(qb, cmat.astype(BF16), preferred_element_type=F32) \
        + jnp.dot(s.astype(BF16), vb, preferred_element_type=F32)
    nvec = n_ref[0]
    den = a * jnp.sum(q * nvec, axis=1, keepdims=True) + jnp.sum(s, axis=1, keepdims=True)
    hloc = num / jnp.maximum(jnp.abs(den), jnp.exp(-mt))
    m_new = mt[t - 1:t, :]
    f_tot = f_col[t - 1:t, :]
    decay = jnp.exp(f_tot + m - m_new)
    ws = jnp.exp(f_tot - f_col + ig_col - m_new)
    kws = k * ws
    c_ref[0] = decay * cmat + lax.dot_general(kws.astype(BF16), vb, (((0,), (0,)), ((), ())),
                                              preferred_element_type=F32)
    n_ref[0] = decay * nvec + jnp.sum(kws, axis=0, keepdims=True)
    m_ref[0] = m_new

    o = hloc[:t_in, :] * _sigmoid(op_ref[...])
    mu = jnp.mean(o, axis=1, keepdims=True)
    var = jnp.mean(jnp.square(o - mu), axis=1, keepdims=True)
    hn = (o - mu) * lax.rsqrt(var + HEAD_NORM_EPS)
    hn_ref[...] = hn * ng_ref[...] + sk_ref[...] * xc[:t_in, :]


def _mlstm_pallas(proj2d, gates2d, bsz, seq, t_in, t, conv_prev, c0, n0, m0, layer, b_gate, conv_w, conv_b,
                  w_q, w_k, w_v, norm_g, skip):
    heads, dh, _ = w_q.shape
    e = heads * dh
    kw = conv_w.shape[0]
    gl = gates2d.shape[1]
    nl = c0.shape[0]
    assert seq % t_in == 0 and t_in <= t and t_in % SUBLANES == 0 and kw - 1 <= min(CONV_HALO, t_in)
    nc = seq // t_in
    rows = lambda blk: pl.BlockSpec((t_in, dh), lambda h, b, c: (b * nc + c, blk(h)))
    per_head_vec = lambda r: pl.BlockSpec((r, dh), lambda h, b, c: (0, h))
    wspec = pl.BlockSpec((1, dh, dh), lambda h, b, c: (h, 0, 0))
    st = lambda shp, off: pl.BlockSpec((1,) + shp, lambda h, b, c: (off + b * heads + h, 0, 0))
    cpspec = lambda off: pl.BlockSpec((1, kw - 1, dh), lambda h, b, c: (off + b, 0, h))
    sbase = layer * bsz * heads
    hn, cs, c_new, n_new, m_new = pl.pallas_call(
        functools.partial(_mlstm_body, t_in=t_in, t=t, heads=heads),
        grid=(heads, bsz, nc),
        in_specs=[rows(lambda h: h), rows(lambda h: 2 * heads + h),
                  pl.BlockSpec((t_in, gl), lambda h, b, c: (b * nc + c, 0)),
                  pl.BlockSpec((1, gl), lambda h, b, c: (0, 0)),
                  per_head_vec(kw), per_head_vec(1), wspec, wspec, wspec, per_head_vec(1), per_head_vec(1),
                  cpspec(layer * bsz), st((dh, dh), sbase), st((1, dh), sbase), st((1, 1), sbase)],
        out_specs=[rows(lambda h: h), cpspec(0), st((dh, dh), 0), st((1, dh), 0), st((1, 1), 0)],
        out_shape=[jax.ShapeDtypeStruct((bsz * seq, e), F32), jax.ShapeDtypeStruct((bsz, kw - 1, e), F32),
                   jax.ShapeDtypeStruct((bsz * heads, dh, dh), F32), jax.ShapeDtypeStruct((bsz * heads, 1, dh), F32),
                   jax.ShapeDtypeStruct((bsz * heads, 1, 1), F32)],
        scratch_shapes=[pltpu.VMEM((CONV_HALO + t, dh), F32)],
        compiler_params=_cparams("arbitrary", "arbitrary", "arbitrary"),
        name="mlstm",
    )(proj2d, proj2d, gates2d, jnp.pad(b_gate, (0, gl - b_gate.shape[0])).reshape(1, gl),
      conv_w, conv_b.reshape(1, e), w_q.astype(BF16), w_k.astype(BF16), w_v.astype(BF16),
      norm_g.reshape(1, e), skip.reshape(1, e), conv_prev.reshape(nl * bsz, kw - 1, e),
      c0.reshape(nl * bsz * heads, dh, dh), n0.reshape(nl * bsz * heads, 1, dh), m0.reshape(nl * bsz * heads, 1, 1))
    return (hn, cs, c_new.reshape(bsz, heads, dh, dh), n_new.reshape(bsz, heads, dh), m_new.reshape(bsz, heads))


def _s5_disc_body(lr_ref, li_ref, ldt_ref, br_ref, bi_ref, ar_ref, ai_ref, bbr_ref, bbi_ref):
    lr = jnp.minimum(lr_ref[...], -1e-4)
    li = li_ref[...]
    dt = jnp.exp(ldt_ref[...])
    mag = jnp.exp(dt * lr)
    a_re, a_im = mag * jnp.cos(dt * li), mag * jnp.sin(dt * li)
    den = lr * lr + li * li
    xr, xi = a_re - 1.0, a_im
    cr = (xr * lr + xi * li) / den
    ci = (xi * lr - xr * li) / den
    ar_ref[...] = a_re
    ai_ref[...] = a_im
    b_r, b_i = br_ref[...], bi_ref[...]
    bbr_ref[...] = cr * b_r - ci * b_i
    bbi_ref[...] = cr * b_i + ci * b_r


def _s5_discretise_pallas(lam_re, lam_im, log_dt, b_re, b_im):
    g, p, c = b_re.shape
    vm = pl.BlockSpec(memory_space=pltpu.VMEM)
    a_re, a_im, bb_re, bb_im = pl.pallas_call(
        _s5_disc_body,
        in_specs=[vm] * 5,
        out_specs=[vm] * 4,
        out_shape=[jax.ShapeDtypeStruct((g, 1, p), F32)] * 2 + [jax.ShapeDtypeStruct((g, c, p), F32)] * 2,
        name="s5_discretise",
    )(lam_re.reshape(g, 1, p), lam_im.reshape(g, 1, p), log_dt.reshape(g, 1, 1),
      jnp.swapaxes(b_re, 1, 2), jnp.swapaxes(b_im, 1, 2))
    return a_re.reshape(g, p), a_im.reshape(g, p), bb_re, bb_im


S5_STRIP_GROUPS = 8


def _gelu_tanh(x):
    return x * (0.5 * (1.0 + jnp.tanh(math.sqrt(2.0 / math.pi) * (x + 0.044715 * (x * x * x)))))


S5_SEGS = SUBLANES
S5_SEG_CHUNK = 512
S5_STRIPS_PER_STEP = 1
S5_UNROLL = 64


def _cmul(ar, ai, br, bi):
    return ar * br - ai * bi, ar * bi + ai * br


def _s5_seg_body(*refs, chained, nsp):
    u_refs, rest = refs[:nsp], refs[nsp:]
    (wbr_ref, wbi_ref, wcr_ref, wci_ref, ar_ref, ai_ref, d_ref, h0r_ref, h0i_ref, g_ref, hr_ref, hi_ref,
     xr_scr, xi_scr, pwr_scr, pwi_scr, apr_scr, api_scr, gp_scr) = rest
    for ss in range(nsp):
        _s5_strip(pl.program_id(2) * nsp + ss, ss, u_refs[ss], wbr_ref, wbi_ref, wcr_ref, wci_ref, ar_ref, ai_ref,
                  d_ref, h0r_ref, h0i_ref, g_ref, hr_ref, hi_ref, xr_scr.at[ss], xi_scr.at[ss],
                  pwr_scr, pwi_scr, apr_scr, api_scr, gp_scr.at[ss], chained=chained)


def _s5_strip(s, ss, u_ref, wbr_ref, wbi_ref, wcr_ref, wci_ref, ar_ref, ai_ref, d_ref, h0r_ref, h0i_ref,
              g_ref, hr_ref, hi_ref, xr_scr, xi_scr, pwr_scr, pwi_scr, apr_scr, api_scr, gp_scr, *, chained):
    nseg = S5_SEGS
    lu, lst = u_ref.shape[1], xr_scr.shape[1]
    ucols, scols = slice(ss * lu, (ss + 1) * lu), slice(ss * lst, (ss + 1) * lst)
    rows = u_ref.shape[0]
    jn = rows // nseg
    ar, ai = ar_ref[s], ai_ref[s]

    @pl.when((pl.program_id(0) == 0) & (pl.program_id(1) == 0))
    def _():
        pr, pi = ar, ai
        for j in range(jn):
            pwr_scr[s, j:j + 1, :] = pr
            pwi_scr[s, j:j + 1, :] = pi
            if j + 1 < jn:
                pr, pi = _cmul(pr, pi, ar, ai)
        qr, qi = jnp.ones_like(ar), jnp.zeros_like(ai)
        for k in range(nseg + 1):
            apr_scr[s, k:k + 1, :] = qr
            api_scr[s, k:k + 1, :] = qi
            qr, qi = _cmul(qr, qi, pr, pi)

    if chained:
        @pl.when(pl.program_id(1) == 0)
        def _():
            hr_ref[s] = h0r_ref[s]
            hi_ref[s] = h0i_ref[s]

    up = jnp.concatenate([u_ref[pl.ds(j, nseg, stride=jn), :] for j in range(jn)], axis=0)
    ub = up.astype(BF16)
    xr_scr[...] = jnp.dot(ub, wbr_ref[s], preferred_element_type=F32)
    xi_scr[...] = jnp.dot(ub, wbi_ref[s], preferred_element_type=F32)
    blk_of = lambda j: pl.ds(pl.multiple_of(j * nseg, nseg), nseg)

    def scan_step(j, h):
        tr, ti = _cmul(ar, ai, h[0], h[1])
        hr, hi = tr + xr_scr[blk_of(j), :], ti + xi_scr[blk_of(j), :]
        xr_scr[blk_of(j), :] = hr
        xi_scr[blk_of(j), :] = hi
        return hr, hi

    zero = jnp.zeros((nseg, ar.shape[1]), F32)
    hr, hi = lax.fori_loop(0, jn, scan_step, (zero, zero), unroll=min(jn, S5_UNROLL))
    anr, ani = pwr_scr[s, jn - 1:jn, :], pwi_scr[s, jn - 1:jn, :]
    if chained:
        sub = lax.broadcasted_iota(jnp.int32, (nseg, 1), 0)
        er, ei, pr, pi = hr, hi, anr, ani
        sh = 1
        while sh < nseg:
            keep = sub >= sh
            tr, ti = _cmul(pr, pi, jnp.where(keep, pltpu.roll(er, sh, 0), 0.0),
                           jnp.where(keep, pltpu.roll(ei, sh, 0), 0.0))
            er, ei = er + tr, ei + ti
            pr, pi = _cmul(pr, pi, pr, pi)
            sh *= 2
        hin_r, hin_i = hr_ref[s], hi_ref[s]
        tr, ti = _cmul(apr_scr[s, 0:nseg, :], api_scr[s, 0:nseg, :], hin_r, hin_i)
        cr = jnp.where(sub >= 1, pltpu.roll(er, 1, 0), 0.0) + tr
        ci = jnp.where(sub >= 1, pltpu.roll(ei, 1, 0), 0.0) + ti
        tr, ti = _cmul(apr_scr[s, 1:nseg + 1, :], api_scr[s, 1:nseg + 1, :], hin_r, hin_i)
        hr_ref[s] = (er + tr)[nseg - 1:nseg, :]
        hi_ref[s] = (ei + ti)[nseg - 1:nseg, :]
    else:
        cr, ci = h0r_ref[:, scols], h0i_ref[:, scols]
        tr, ti = _cmul(anr, ani, cr, ci)
        hr_ref[:, scols] = hr + tr
        hi_ref[:, scols] = hi + ti

    def fix_step(j, carry):
        tr, ti = _cmul(pwr_scr[s, pl.ds(j, 1), :], pwi_scr[s, pl.ds(j, 1), :], cr, ci)
        xr_scr[blk_of(j), :] = xr_scr[blk_of(j), :] + tr
        xi_scr[blk_of(j), :] = xi_scr[blk_of(j), :] + ti
        return carry

    lax.fori_loop(0, jn, fix_step, 0, unroll=min(jn, S5_UNROLL))
    y = (jnp.dot(xr_scr[...].astype(BF16), wcr_ref[s], preferred_element_type=F32)
         - jnp.dot(xi_scr[...].astype(BF16), wci_ref[s], preferred_element_type=F32))
    g = _gelu_tanh(y + d_ref[:, ucols] * up)
    for j in range(jn):
        gp_scr[pl.ds(j, nseg, stride=jn), :] = g[j * nseg:(j + 1) * nseg, :]
    g_ref[:, ucols] = gp_scr[...]


def _s5_seg_pallas(proj2d, bsz, seq, chunk, h0_re, h0_im, a_re, a_im, bb_re, bb_im, c_re, c_im, d_skip):
    g, cch, p = bb_re.shape
    e = g * cch
    sg = S5_STRIP_GROUPS
    ns = g // sg
    lu, ls = sg * cch, sg * p
    assert g % sg == 0 and lu == LANES
    chained = seq % chunk == 0
    if chained:
        rows, n0, nc = chunk, bsz, seq // chunk
    else:
        assert bsz % S5_SEGS == 0
        rows, n0, nc = S5_SEGS * seq, bsz // S5_SEGS, 1
    jn = rows // S5_SEGS
    eye = jnp.eye(sg, dtype=F32)
    wb = lambda bb: jnp.einsum('sgcp,gh->sgchp', bb.reshape(ns, sg, cch, p), eye).reshape(ns, lu, ls).astype(BF16)
    wc = lambda cc: jnp.einsum('sgcp,gh->sgphc', cc.reshape(ns, sg, cch, p), eye).reshape(ns, ls, lu).astype(BF16)
    const3 = lambda shp: pl.BlockSpec(shp, lambda b, c, s: (0, 0, 0))
    nsp = S5_STRIPS_PER_STEP
    assert ns % nsp == 0
    if chained:
        hspec = pl.BlockSpec((ns, 1, ls), lambda b, c, s: (b, 0, 0))
        hshape = (bsz * ns, 1, ls)
    else:
        hspec = pl.BlockSpec((S5_SEGS, nsp * ls), lambda b, c, s: (b, s))
        hshape = (bsz, ns * ls)
    uspec = lambda ss: pl.BlockSpec((rows, lu), lambda b, c, s: (b * nc + c, s * nsp + ss))
    gout, hr, hi = pl.pallas_call(
        functools.partial(_s5_seg_body, chained=chained, nsp=nsp),
        grid=(n0, nc, ns // nsp),
        in_specs=[uspec(ss) for ss in range(nsp)]
        + [const3((ns, lu, ls)), const3((ns, lu, ls)), const3((ns, ls, lu)), const3((ns, ls, lu)),
           const3((ns, 1, ls)), const3((ns, 1, ls)),
           pl.BlockSpec((1, nsp * lu), lambda b, c, s: (0, s)), hspec, hspec],
        out_specs=[pl.BlockSpec((rows, nsp * lu), lambda b, c, s: (b * nc + c, s)), hspec, hspec],
        out_shape=[jax.ShapeDtypeStruct((bsz * seq, e), F32),
                   jax.ShapeDtypeStruct(hshape, F32), jax.ShapeDtypeStruct(hshape, F32)],
        scratch_shapes=[pltpu.VMEM((nsp, rows, ls), F32), pltpu.VMEM((nsp, rows, ls), F32),
                        pltpu.VMEM((ns, jn, ls), F32), pltpu.VMEM((ns, jn, ls), F32),
                        pltpu.VMEM((ns, 2 * S5_SEGS, ls), F32), pltpu.VMEM((ns, 2 * S5_SEGS, ls), F32),
                        pltpu.VMEM((nsp, rows, lu), F32)],
        compiler_params=_cparams("arbitrary", "arbitrary", "arbitrary"),
        name="s5_scan",
    )(*([proj2d] * nsp), wb(bb_re), wb(bb_im), wc(c_re), wc(c_im),
      a_re.reshape(ns, 1, ls), a_im.reshape(ns, 1, ls), d_skip.reshape(1, e),
      h0_re.reshape(hshape), h0_im.reshape(hshape))
    return gout, hr.reshape(bsz, g, p), hi.reshape(bsz, g, p)


def _glu_body(g_ref, w_ref, o_ref):
    g = g_ref[...]
    o_ref[...] = g * _sigmoid(jnp.dot(g.astype(BF16), w_ref[...], preferred_element_type=F32))


def _glu(g2d, w_bf16, tm):
    m, e = g2d.shape
    tm = min(tm, m)
    assert m % tm == 0 and w_bf16.shape == (e, e)
    return pl.pallas_call(
        _glu_body,
        grid=(m // tm,),
        in_specs=[pl.BlockSpec((tm, e), lambda i: (i, 0)),
                  pl.BlockSpec((e, e), lambda i: (0, 0))],
        out_specs=pl.BlockSpec((tm, e), lambda i: (i, 0)),
        out_shape=jax.ShapeDtypeStruct((m, e), F32),
        compiler_params=_cparams("arbitrary"),
        name="glu",
    )(g2d, w_bf16)


POOL_WINDOWS = (2, 4, 8, 16)
POOL_HALO = 2 * SUBLANES
POOL_CHUNK = 512


def _pool_body(u_ref, pre_ref, w_ref, sc_ref, o_ref, st_ref, ext, *, windows, start):
    c = pl.program_id(1)
    t = u_ref.shape[0]
    dp = max(windows) - 1
    dg = u_ref.shape[1] // len(windows)

    @pl.when(c == 0)
    def _():
        ext[POOL_HALO - dp:POOL_HALO, :] = pre_ref[0]

    @pl.when(c > 0)
    def _():
        ext[POOL_HALO - dp:POOL_HALO, :] = ext[POOL_HALO + t - dp:POOL_HALO + t, :]

    ext[POOL_HALO:POOL_HALO + t, :] = u_ref[...]
    st_ref[0] = ext[POOL_HALO + t - dp:POOL_HALO + t, :]
    pos = start + c * t + lax.broadcasted_iota(jnp.int32, (t, 1), 0)
    for g, w in enumerate(windows):
        lo = g * dg
        cur = ext[POOL_HALO:POOL_HALO + t, lo:lo + dg]
        tot = cur
        for j in range(1, w):
            tot = tot + ext[POOL_HALO - j:POOL_HALO - j + t, lo:lo + dg]
        cnt = jnp.minimum(pos + 1, w).astype(F32)
        mix = (tot / cnt - cur).astype(BF16)
        o_ref[:, lo:lo + dg] = jnp.dot(mix, w_ref[g], preferred_element_type=F32) * sc_ref[:, lo:lo + dg]


def _pool_pallas(proj2d, bsz, seq, chunk, prefix, start, w_grp, scale, windows):
    nw, dg, _ = w_grp.shape
    e = nw * dg
    dp = max(windows) - 1
    assert seq % chunk == 0 and prefix.shape == (bsz, dp, e) and dp < POOL_HALO and nw == len(windows)
    nc = seq // chunk
    return pl.pallas_call(
        functools.partial(_pool_body, windows=windows, start=start),
        grid=(bsz, nc),
        in_specs=[pl.BlockSpec((chunk, e), lambda b, c: (b * nc + c, 0)),
                  pl.BlockSpec((1, dp, e), lambda b, c: (b, 0, 0)),
                  pl.BlockSpec((nw, dg, dg), lambda b, c: (0, 0, 0)),
                  pl.BlockSpec((1, e), lambda b, c: (0, 0))],
        out_specs=[pl.BlockSpec((chunk, e), lambda b, c: (b * nc + c, 0)),
                   pl.BlockSpec((1, dp, e), lambda b, c: (b, 0, 0))],
        out_shape=[jax.ShapeDtypeStruct((bsz * seq, e), F32), jax.ShapeDtypeStruct((bsz, dp, e), F32)],
        scratch_shapes=[pltpu.VMEM((POOL_HALO + chunk, e), F32)],
        compiler_params=_cparams("arbitrary", "arbitrary"),
        name="pool",
    )(proj2d, prefix, w_grp.astype(BF16), scale.reshape(1, e))


DIL_PATTERNS = ((128, 1), (512, 4), (2048, 16))
DIL_BLOCK = 128
DIL_TQ = 2048
DIL_UNROLL = 8
DIL_MERGE_ROWS = 256


def _dil_prompt_body(*refs, dils, dk, nvp):
    q0_ref, q1_ref, q2_ref, k0_ref, k1_ref, k2_ref = refs[:6]
    v_refs = refs[6:6 + nvp]
    o_ref = refs[6 + nvp]
    per_pat = nvp + 2
    scr = refs[7 + nvp:]
    pat_scr = [scr[g * per_pat:(g + 1) * per_pat] for g in range(len(dils))]
    h, i = pl.program_id(1), pl.program_id(2)
    tq = o_ref.shape[0]
    qb = DIL_BLOCK
    nsub = tq // qb
    lane = lax.broadcasted_iota(jnp.int32, (1, LANES), 1)
    mine = (lane // dk) == (h % (LANES // dk))
    rowi = lax.broadcasted_iota(jnp.int32, (qb, 1), 0)
    coli = lax.broadcasted_iota(jnp.int32, (1, qb), 1)
    scale = dk ** -0.5
    for g, d in enumerate(dils):
        q_ref, k_ref = (q0_ref, q1_ref, q2_ref)[g], (k0_ref, k1_ref, k2_ref)[g]
        ld = d.bit_length() - 1

        acc, m_scr, l_scr = pat_scr[g][:nvp], pat_scr[g][nvp], pat_scr[g][nvp + 1]

        def body(idx, carry, q_ref=q_ref, k_ref=k_ref, d=d, ld=ld, acc=acc, m_scr=m_scr, l_scr=l_scr):
            r, bl = idx & (d - 1), idx >> ld
            lstart = r + (d * qb) * bl
            gstart = i * tq + lstart
            has_prev = gstart >= d * qb
            pstart = jnp.where(has_prev, gstart - d * qb, gstart)
            rows = lambda s: pl.ds(s, qb, stride=d) if d > 1 else pl.ds(s, qb)
            qm = (jnp.where(mine, q_ref[rows(lstart), :], 0.0) * scale).astype(BF16)
            sc = _dot_nt(qm, k_ref[rows(gstart), :].astype(BF16))
            sp = _dot_nt(qm, k_ref[rows(pstart), :].astype(BF16))
            sc = jnp.where(coli <= rowi, sc, -jnp.inf)
            sp = jnp.where((coli >= rowi) & has_prev, sp, -jnp.inf)
            mb = jnp.max(jnp.maximum(sc, sp), axis=1, keepdims=True)
            pc, pp = jnp.exp(sc - mb), jnp.exp(sp - mb)
            lb = jnp.sum(pc + pp, axis=1, keepdims=True)
            vrows = lambda s: jnp.concatenate([v[rows(s), :] for v in v_refs], axis=1).astype(BF16)
            nb = (jnp.dot(pc.astype(BF16), vrows(gstart), preferred_element_type=F32)
                  + jnp.dot(pp.astype(BF16), vrows(pstart), preferred_element_type=F32))
            for p, a in enumerate(acc):
                a[rows(lstart), :] = nb[:, p * LANES:(p + 1) * LANES]
            m_scr[rows(lstart), :] = jnp.broadcast_to(mb, (qb, LANES))
            l_scr[rows(lstart), :] = jnp.broadcast_to(lb, (qb, LANES))
            return carry

        lax.fori_loop(0, nsub, body, 0, unroll=DIL_UNROLL)

    def merge(c, carry):
        rs = pl.ds(pl.multiple_of(c * DIL_MERGE_ROWS, DIL_MERGE_ROWS), DIL_MERGE_ROWS)
        ms = [ps[nvp][rs, :] for ps in pat_scr]
        mmax = functools.reduce(jnp.maximum, ms)
        es = [jnp.exp(m - mmax) for m in ms]
        den = sum(ps[nvp + 1][rs, :] * e for ps, e in zip(pat_scr, es))
        for p in range(nvp):
            o_ref[rs, p * LANES:(p + 1) * LANES] = sum(ps[p][rs, :] * e for ps, e in zip(pat_scr, es)) / den
        return carry

    lax.fori_loop(0, tq // DIL_MERGE_ROWS, merge, 0)


def _dil_prompt_pallas(proj2d, bsz, seq, heads, dk, dv, col_q, col_k, col_v):
    dils = tuple(d for _, d in DIL_PATTERNS)
    assert all(w == d * DIL_BLOCK for w, d in DIL_PATTERNS)
    tq = min(DIL_TQ, seq)
    assert seq % tq == 0 and tq % (DIL_BLOCK * max(dils)) == 0 and LANES % dk == 0 and dv % LANES == 0
    nq = seq // tq
    hpb = LANES // dk
    nqk = heads * dk
    nvp = dv // LANES
    qspec = lambda g: pl.BlockSpec((tq, LANES), lambda b, h, i: (b * nq + i, (col_q + g * nqk) // LANES + h // hpb))
    kspec = lambda g: pl.BlockSpec((seq, LANES), lambda b, h, i: (b, (col_k + g * nqk) // LANES + h // hpb))
    vspec = lambda p: pl.BlockSpec((seq, LANES), lambda b, h, i: (b, col_v // LANES + h * nvp + p))
    return pl.pallas_call(
        functools.partial(_dil_prompt_body, dils=dils, dk=dk, nvp=nvp),
        grid=(bsz, heads, nq),
        in_specs=[qspec(0), qspec(1), qspec(2), kspec(0), kspec(1), kspec(2)] + [vspec(p) for p in range(nvp)],
        out_specs=pl.BlockSpec((tq, dv), lambda b, h, i: (b * nq + i, h)),
        out_shape=jax.ShapeDtypeStruct((bsz * seq, heads * dv), F32),
        scratch_shapes=[pltpu.VMEM((tq, LANES), F32)] * ((2 + nvp) * len(dils)),
        compiler_params=_cparams("arbitrary", "arbitrary", "arbitrary"),
        name="dilated_prompt",
    )(*([proj2d] * (6 + nvp)))


DIL_S_ROWS = 512


def _dil_decode_body(q_ref, kn_ref, vn_ref, k1_ref, k2_ref, k3_ref, v_ref, k3x_ref, vx_ref,
                     o_ref, k1o_ref, k2o_ref, k3o_ref, vo_ref, m_scr, l_scr, acc_scr, *, dils, dk):
    c, nch = pl.program_id(1), pl.num_programs(1)
    ls = q_ref.shape[2]
    d2, d3 = dils[1], dils[2]
    rc = v_ref.shape[1] * v_ref.shape[2]
    scale = dk ** -0.5
    lanes = lambda x: jnp.broadcast_to(x, x.shape[:-1] + (LANES,))
    scores = lambda ks, q: jnp.sum(ks * q, axis=-1, keepdims=True)

    @pl.when(c == 0)
    def _():
        m_scr[...] = jnp.full(m_scr.shape, -jnp.inf, F32)
        l_scr[...] = jnp.zeros(l_scr.shape, F32)
        acc_scr[...] = jnp.zeros(acc_scr.shape, F32)

    for i in range(ls):
        q = q_ref[0, 2, i] * scale
        s = scores(k3_ref[0, :, i], q)
        m_old = m_scr[i][:, 0:1]
        m_new = jnp.maximum(m_old, jnp.max(s, axis=0))
        alpha = jnp.exp(m_old - m_new)
        e = jnp.exp(s - m_new)
        m_scr[i] = lanes(m_new)
        l_scr[i] = lanes(l_scr[i][:, 0:1] * alpha + jnp.sum(e, axis=0))
        acc_scr[i] = acc_scr[i] * alpha + jnp.sum(e * v_ref[0, :, i], axis=0)

    hk, hv = k3_ref.shape[3:], v_ref.shape[3:]
    k3o_ref[0, 0:rc - ls] = k3_ref[0].reshape((rc,) + hk)[ls:rc]
    vo_ref[0, 0:rc - ls] = v_ref[0].reshape((rc,) + hv)[ls:rc]

    @pl.when(c < nch - 1)
    def _():
        k3o_ref[0, rc - ls:rc] = k3x_ref[0, 0]
        vo_ref[0, rc - ls:rc] = vx_ref[0, 0]

    @pl.when(c == nch - 1)
    def _():
        k3o_ref[0, rc - ls:rc] = kn_ref[0, 2]
        vo_ref[0, rc - ls:rc] = vn_ref[0]
        l1, l2 = k1_ref.shape[1], k2_ref.shape[1] * d2
        k1o_ref[0, 0:l1 - ls] = k1_ref[0, ls:l1]
        k1o_ref[0, l1 - ls:l1] = kn_ref[0, 0]
        k2o_ref[0, 0:l2 - ls] = k2_ref[0].reshape((l2,) + hk)[ls:l2]
        k2o_ref[0, l2 - ls:l2] = kn_ref[0, 1]
        na = v_ref.shape[1]
        v2 = lambda r: jnp.stack([v_ref[0, na - l2 // d3:na, r + d2 * t] for t in range(d3 // d2)], axis=1) \
            .reshape((l2 // d2,) + hv)
        v1 = v_ref[0, na - l1 // d3:na].reshape((l1,) + hv)
        for i in range(ls):
            parts = []
            q = q_ref[0, 0, i] * scale
            ks = jnp.concatenate([k1_ref[0, i:l1], kn_ref[0, 0, 0:i + 1]], axis=0)
            vs = jnp.concatenate([v1[i:l1], vn_ref[0, 0:i + 1]], axis=0)
            parts.append((scores(ks, q), vs))
            q = q_ref[0, 1, i] * scale
            news = list(range(i % d2, i + 1, d2))
            ks = jnp.concatenate([k2_ref[0, i // d2:, i % d2]] + [kn_ref[0, 1, t:t + 1] for t in news], axis=0)
            vs = jnp.concatenate([v2(i % d2)[i // d2:]] + [vn_ref[0, t:t + 1] for t in news], axis=0)
            parts.append((scores(ks, q), vs))
            q = q_ref[0, 2, i] * scale
            s_new = scores(kn_ref[0, 2, i:i + 1], q)[0]
            m_old = m_scr[i][:, 0:1]
            m3 = jnp.maximum(m_old, s_new)
            alpha, e_new = jnp.exp(m_old - m3), jnp.exp(s_new - m3)
            den3 = l_scr[i][:, 0:1] * alpha + e_new
            num3 = acc_scr[i] * alpha + e_new * vn_ref[0, i]
            stats = []
            for s, vs in parts:
                m = jnp.max(s, axis=0)
                e = jnp.exp(s - m)
                stats.append((jnp.sum(e * vs, axis=0), m, jnp.sum(e, axis=0)))
            stats.append((num3, m3, den3))
            mmax = functools.reduce(jnp.maximum, [m for _, m, _ in stats])
            num = sum(nu * jnp.exp(m - mmax) for nu, m, _ in stats)
            den = sum(de * jnp.exp(m - mmax) for _, m, de in stats)
            o_ref[0, i] = num / den


def _dil_decode_pallas(proj2d, bsz, ls, k_caches, v_cache, layer, col_q, col_k, col_v):
    heads, dk = k_caches[0].shape[3], k_caches[0].shape[4]
    dv = v_cache.shape[4]
    nl = v_cache.shape[0]
    wins, dils = tuple(w for w, _ in DIL_PATTERNS), tuple(d for _, d in DIL_PATTERNS)
    l1, l2, l3 = (kc.shape[2] for kc in k_caches)
    lv = v_cache.shape[2]
    d1, d2, d3 = dils
    rc = DIL_S_ROWS
    assert (l1, l2, l3) == wins and lv == l3 and d1 == 1 and d3 % d2 == 0 and ls == SUBLANES and ls <= d3
    assert lv % rc == 0 and rc % d3 == 0 and l2 <= rc and l1 <= rc and l1 % d3 == 0 and l2 % d3 == 0
    nch = lv // rc
    npat = len(dils)
    nqk = heads * dk
    p3 = proj2d.reshape(bsz, ls, -1)
    pick = lambda col: jnp.swapaxes(p3[:, :, col:col + npat * nqk].reshape(bsz, ls, npat, heads, dk), 1, 2)
    q5, kn5 = pick(col_q), pick(col_k)
    vn4 = p3[:, :, col_v:col_v + heads * dv].reshape(bsz, ls, heads, dv)
    base = layer * bsz
    k1 = k_caches[0].reshape(nl * bsz, l1, heads, dk)
    k2 = k_caches[1].reshape(nl * bsz, l2 // d2, d2, heads, dk)
    k3 = k_caches[2].reshape(nl * bsz, l3 // d3, d3, heads, dk)
    v6 = v_cache.reshape(nl * bsz, lv // d3, d3, heads, dv)
    k3x = k_caches[2].reshape(nl * bsz, l3 // ls, ls, heads, dk)
    vx = v_cache.reshape(nl * bsz, lv // ls, ls, heads, dv)
    nxt = lambda b, c: (base + b, jnp.minimum((c + 1) * (rc // ls), lv // ls - 1), 0, 0, 0)
    bc4 = lambda shp: pl.BlockSpec((1,) + shp, lambda b, c: (b, 0, 0, 0))
    outs = pl.pallas_call(
        functools.partial(_dil_decode_body, dils=dils, dk=dk),
        grid=(bsz, nch),
        in_specs=[pl.BlockSpec((1, npat, ls, heads, dk), lambda b, c: (b, 0, 0, 0, 0)),
                  pl.BlockSpec((1, npat, ls, heads, dk), lambda b, c: (b, 0, 0, 0, 0)),
                  bc4((ls, heads, dv)),
                  pl.BlockSpec((1, l1, heads, dk), lambda b, c: (base + b, 0, 0, 0)),
                  pl.BlockSpec((1, l2 // d2, d2, heads, dk), lambda b, c: (base + b, 0, 0, 0, 0)),
                  pl.BlockSpec((1, rc // d3, d3, heads, dk), lambda b, c: (base + b, c, 0, 0, 0)),
                  pl.BlockSpec((1, rc // d3, d3, heads, dv), lambda b, c: (base + b, c, 0, 0, 0)),
                  pl.BlockSpec((1, 1, ls, heads, dk), nxt),
                  pl.BlockSpec((1, 1, ls, heads, dv), nxt)],
        out_specs=[bc4((ls, heads, dv)), bc4((l1, heads, dk)), bc4((l2, heads, dk)),
                   pl.BlockSpec((1, rc, heads, dk), lambda b, c: (b, c, 0, 0)),
                   pl.BlockSpec((1, rc, heads, dv), lambda b, c: (b, c, 0, 0))],
        out_shape=[jax.ShapeDtypeStruct((bsz, ls, heads, dv), F32), jax.ShapeDtypeStruct((bsz, l1, heads, dk), F32),
                   jax.ShapeDtypeStruct((bsz, l2, heads, dk), F32), jax.ShapeDtypeStruct((bsz, l3, heads, dk), F32),
                   jax.ShapeDtypeStruct((bsz, lv, heads, dv), F32)],
        scratch_shapes=[pltpu.VMEM((ls, heads, LANES), F32), pltpu.VMEM((ls, heads, LANES), F32),
                        pltpu.VMEM((ls, heads, dv), F32)],
        compiler_params=_cparams("arbitrary", "arbitrary"),
        name="dilated_decode",
    )(q5, kn5, vn4, k1, k2, k3, v6, k3x, vx)
    o, nk1, nk2, nk3, nv = outs
    return o.reshape(bsz * ls, heads * dv), (nk1, nk2, nk3), nv


def _mlstm_chunks(seq):
    if seq % MLSTM_CHUNK == 0:
        return MLSTM_CHUNK, MLSTM_CHUNK
    return seq, max(MLSTM_MIN_CHUNK, seq)


def _chunk_of(seq, chunk):
    return chunk if seq % chunk == 0 else seq


def kernel(x_prompt, x_sample, state_mlstm_c, state_mlstm_n, state_mlstm_m, state_mlstm_conv, state_s5_re, state_s5_im, cache_dil_k1, cache_dil_k2, cache_dil_k3, cache_dil_v, state_pool, norm_g, final_norm_g, a_w_in, a_b_gate, a_conv_w, a_conv_b, a_w_q, a_w_k, a_w_v, a_norm_g, a_skip, a_w_out, b_w_in, b_lam_re, b_lam_im, b_log_dt, b_B_re, b_B_im, b_C_re, b_C_im, b_d, b_w_glu, b_w_out, c_w_in, c_w_out, d_w_in, d_w_grp, d_scale, d_w_out):
    bp, lp, dm = x_prompt.shape
    bs, ls, _ = x_sample.shape
    depth = norm_g.shape[0]
    H, Dh = a_w_q.shape[1], a_w_q.shape[2]
    E = H * Dh
    heads, dk = cache_dil_k1.shape[3], cache_dil_k1.shape[4]
    dv = cache_dil_v.shape[4]
    nqk = heads * dk
    npat = len(DIL_PATTERNS)
    names = ('mlstm_c', 'mlstm_n', 'mlstm_m', 'mlstm_conv', 's5_re', 's5_im', 'k1', 'k2', 'k3', 'v', 'pool')
    new_p = {nm: [] for nm in names}
    new_s = {nm: [] for nm in names}
    yp = x_prompt.reshape(bp * lp, dm)
    ys = x_sample.reshape(bs * ls, dm)
    zeros = lambda *shape: jnp.zeros(shape, F32)

    def proj_both(layer, w):
        wb = w.astype(BF16)
        return (_norm_matmul(yp, norm_g[layer], wb, PROJ_TM, PROJ_TN),
                _norm_matmul(ys, norm_g[layer], wb, PROJ_TM, PROJ_TN))

    for layer in range(depth):
        kind, j = layer % N_MIXERS, layer // N_MIXERS
        fg = final_norm_g if layer == depth - 1 else None
        if kind == 0:
            pp, ps = proj_both(layer, a_w_in[j][:, :3 * E])
            gp, gs = proj_both(layer, jnp.pad(a_w_in[j][:, 3 * E:], ((0, 0), (0, LANES - 2 * H))))
            w = (a_b_gate[j], a_conv_w[j], a_conv_b[j], a_w_q[j], a_w_k[j], a_w_v[j], a_norm_g[j], a_skip[j])
            ap, *sp = _mlstm_pallas(pp, gp, bp, lp, *_mlstm_chunks(lp), zeros(1, bp, a_conv_w.shape[1] - 1, E),
                                    zeros(1, bp, H, Dh, Dh), zeros(1, bp, H, Dh), zeros(1, bp, H), 0, *w)
            as_, *ss = _mlstm_pallas(ps, gs, bs, ls, *_mlstm_chunks(ls), state_mlstm_conv, state_mlstm_c,
                                     state_mlstm_n, state_mlstm_m, j, *w)
            keys = ('mlstm_conv', 'mlstm_c', 'mlstm_n', 'mlstm_m')
            zblk, w_out = 1, a_w_out[j]
        elif kind == 1:
            pp, ps = proj_both(layer, b_w_in[j])
            a_re, a_im, bb_re, bb_im = _s5_discretise_pallas(b_lam_re[j], b_lam_im[j], b_log_dt[j], b_B_re[j], b_B_im[j])
            w = (a_re, a_im, bb_re, bb_im, b_C_re[j], b_C_im[j], b_d[j])
            zs = zeros(bp, *state_s5_re.shape[2:])
            gp_, *sp = _s5_seg_pallas(pp, bp, lp, S5_SEG_CHUNK, zs, zs, *w)
            gs_, *ss = _s5_seg_pallas(ps, bs, ls, S5_SEG_CHUNK, state_s5_re[j], state_s5_im[j], *w)
            wglu = b_w_glu[j].astype(BF16)
            ap, as_ = _glu(gp_, wglu, GLU_TM), _glu(gs_, wglu, GLU_TM)
            keys = ('s5_re', 's5_im')
            zblk, w_out = 1, b_w_out[j]
        elif kind == 2:
            wc = c_w_in[j]
            wc = jnp.concatenate([wc[:, 2 * npat * nqk + E:], wc[:, 2 * npat * nqk:2 * npat * nqk + E],
                                  wc[:, :2 * npat * nqk]], axis=1)
            col_v, col_q, col_k = E, 2 * E, 2 * E + npat * nqk
            pp, ps = proj_both(layer, wc)
            ap = _dil_prompt_pallas(pp, bp, lp, heads, dk, dv, col_q, col_k, col_v)
            pp3 = pp.reshape(bp, lp, -1)
            sp = [pp3[:, lp - min(win, lp):, col_k + g * nqk:col_k + (g + 1) * nqk].reshape(bp, -1, heads, dk)
                  for g, (win, _) in enumerate(DIL_PATTERNS)]
            sp.append(pp3[:, lp - min(DIL_PATTERNS[-1][0], lp):, col_v:col_v + E].reshape(bp, -1, heads, dv))
            as_, kq, vq = _dil_decode_pallas(ps, bs, ls, (cache_dil_k1, cache_dil_k2, cache_dil_k3), cache_dil_v, j,
                                             col_q, col_k, col_v)
            ss = (*kq, vq)
            keys = ('k1', 'k2', 'k3', 'v')
            zblk, w_out = 0, c_w_out[j]
        else:
            pp, ps = proj_both(layer, d_w_in[j])
            dp = max(POOL_WINDOWS) - 1
            ap, *sp = _pool_pallas(pp, bp, lp, _chunk_of(lp, POOL_CHUNK), zeros(bp, dp, E), 0, d_w_grp[j], d_scale[j],
                                   POOL_WINDOWS)
            as_, *ss = _pool_pallas(ps, bs, ls, _chunk_of(ls, POOL_CHUNK), state_pool[j], PAST_LEN, d_w_grp[j],
                                    d_scale[j], POOL_WINDOWS)
            keys = ('pool',)
            zblk, w_out = 1, d_w_out[j]
        for nm, a, b in zip(keys, sp, ss):
            new_p[nm].append(a)
            new_s[nm].append(b)
        wo = w_out.astype(BF16)
        yp = _gated_out(yp, (ap, 0), (pp, zblk), wo, OUT_TM, fg)
        ys = _gated_out(ys, (as_, 0), (ps, zblk), wo, OUT_TM, fg)
    out = [yp.reshape(bp, lp, dm), ys.reshape(bs, ls, dm)]
    for nm in names:
        out.append(jnp.stack(new_p[nm]))
        out.append(jnp.stack(new_s[nm]))
    return tuple(out)
```

```python
import functools
import math

import jax
import jax.numpy as jnp
from jax import lax
from jax.experimental import pallas as pl
from jax.experimental.pallas import tpu as pltpu

F32 = jnp.float32
BF16 = jnp.bfloat16
RMS_EPS = 1e-6
HEAD_NORM_EPS = 1e-6
LANES = 128
SUBLANES = 8
VMEM_LIMIT = 48 * 1024 * 1024
PAST_LEN = 8192
N_MIXERS = 4
PROJ_TM, PROJ_TN = 2048, 1024
OUT_TM = 512
GLU_TM = 512


def _cparams(*sem):
    return pltpu.CompilerParams(dimension_semantics=sem, vmem_limit_bytes=VMEM_LIMIT)


def _sigmoid(x):
    return 1.0 / (1.0 + jnp.exp(-x))


def _dot_nt(a, b):
    return lax.dot_general(a, b, (((1,), (1,)), ((), ())), preferred_element_type=F32)


def _norm_matmul_body(x_ref, g_ref, w_ref, o_ref, h_scr):
    @pl.when(pl.program_id(1) == 0)
    def _():
        xf = x_ref[...]
        ms = jnp.mean(xf * xf, axis=-1, keepdims=True)
        h_scr[...] = (xf * lax.rsqrt(ms + RMS_EPS) * g_ref[...]).astype(BF16)

    o_ref[...] = jnp.dot(h_scr[...], w_ref[...], preferred_element_type=F32)


def _norm_matmul(x2d, g, w_bf16, tm, tn):
    m, k = x2d.shape
    n = w_bf16.shape[1]
    tm, tn = min(tm, m), min(tn, n)
    assert m % tm == 0 and n % tn == 0
    return pl.pallas_call(
        _norm_matmul_body,
        grid=(m // tm, n // tn),
        in_specs=[pl.BlockSpec((tm, k), lambda i, j: (i, 0)),
                  pl.BlockSpec((1, k), lambda i, j: (0, 0)),
                  pl.BlockSpec((k, tn), lambda i, j: (0, j))],
        out_specs=pl.BlockSpec((tm, tn), lambda i, j: (i, j)),
        out_shape=jax.ShapeDtypeStruct((m, n), F32),
        scratch_shapes=[pltpu.VMEM((tm, k), BF16)],
        compiler_params=_cparams("arbitrary", "arbitrary"),
        name="norm_matmul",
    )(x2d, g.reshape(1, k), w_bf16)


def _gated_out_body(*refs, final):
    if final:
        x_ref, a_ref, z_ref, w_ref, fg_ref, o_ref = refs
    else:
        x_ref, a_ref, z_ref, w_ref, o_ref = refs
    z = z_ref[...]
    act = (a_ref[...] * (z * _sigmoid(z))).astype(BF16)
    y = x_ref[...] + jnp.dot(act, w_ref[...], preferred_element_type=F32)
    if final:
        ms = jnp.mean(y * y, axis=-1, keepdims=True)
        y = y * lax.rsqrt(ms + RMS_EPS) * fg_ref[...]
    o_ref[...] = y


def _gated_out(x2d, a_src, z_src, w_bf16, tm, final_g=None):
    m, d = x2d.shape
    e = w_bf16.shape[0]
    tm = min(tm, m)
    assert m % tm == 0
    (a_arr, a_blk), (z_arr, z_blk) = a_src, z_src
    in_specs = [pl.BlockSpec((tm, d), lambda i: (i, 0)),
                pl.BlockSpec((tm, e), lambda i: (i, a_blk)),
                pl.BlockSpec((tm, e), lambda i: (i, z_blk)),
                pl.BlockSpec((e, d), lambda i: (0, 0))]
    args = [x2d, a_arr, z_arr, w_bf16]
    if final_g is not None:
        in_specs.append(pl.BlockSpec((1, d), lambda i: (0, 0)))
        args.append(final_g.reshape(1, d))
    return pl.pallas_call(
        functools.partial(_gated_out_body, final=final_g is not None),
        grid=(m // tm,),
        in_specs=in_specs,
        out_specs=pl.BlockSpec((tm, d), lambda i: (i, 0)),
        out_shape=jax.ShapeDtypeStruct((m, d), F32),
        compiler_params=_cparams("arbitrary"),
        name="gated_out",
    )(*args)


MLSTM_CHUNK = 256
MLSTM_MIN_CHUNK = 128
CONV_HALO = SUBLANES


def _log_sigmoid(x):
    return jnp.minimum(x, 0.0) - jnp.log1p(jnp.exp(-jnp.abs(x)))


def _mlstm_body(xm_ref, op_ref, gt_ref, bg_ref, cw_ref, cb_ref, wq_ref, wk_ref, wv_ref, ng_ref, sk_ref,
                cp_ref, c0_ref, n0_ref, m0_ref,
                hn_ref, cs_ref, c_ref, n_ref, m_ref, ext, *, t_in, t, heads):
    h, c = pl.program_id(0), pl.program_id(2)
    kw = cw_ref.shape[0]
    dh = xm_ref.shape[1]
    lo = CONV_HALO - (kw - 1)

    @pl.when(c == 0)
    def _():
        c_ref[...] = c0_ref[...]
        n_ref[...] = n0_ref[...]
        m_ref[...] = m0_ref[...]
        ext[lo:CONV_HALO, :] = cp_ref[0]
        if t_in < t:
            ext[CONV_HALO + t_in:CONV_HALO + t, :] = jnp.zeros((t - t_in, dh), F32)

    @pl.when(c > 0)
    def _():
        ext[lo:CONV_HALO, :] = ext[lo + t_in:CONV_HALO + t_in, :]

    ext[CONV_HALO:CONV_HALO + t_in, :] = xm_ref[...]
    cs_ref[0] = ext[lo + t_in:CONV_HALO + t_in, :]

    xm = ext[CONV_HALO:CONV_HALO + t, :]
    xconv = cb_ref[...] + ext[lo:lo + t, :] * cw_ref[0:1, :]
    for i in range(1, kw):
        xconv = xconv + ext[lo + i:lo + i + t, :] * cw_ref[i:i + 1, :]
    xc = xconv * _sigmoid(xconv)
    xcb = xc.astype(BF16)
    q = jnp.dot(xcb, wq_ref[0], preferred_element_type=F32)
    k = jnp.dot(xcb, wk_ref[0], preferred_element_type=F32) * (dh ** -0.5)
    v = jnp.dot(xm.astype(BF16), wv_ref[0], preferred_element_type=F32)

    gt = gt_ref[...] + bg_ref[...]
    row = lax.broadcasted_iota(jnp.int32, (t, 1), 0)
    if t_in < t:
        gt = jnp.concatenate([gt, jnp.zeros((t - t_in, gt.shape[1]), F32)], axis=0)
        valid = row < t_in
        ig_all = jnp.where(valid, gt, -jnp.inf)
        f_all = jnp.where(valid, _log_sigmoid(gt), 0.0)
    else:
        ig_all = gt
        f_all = _log_sigmoid(gt)
    sh = 1
    while sh < t:
        f_all = f_all + jnp.where(row >= sh, pltpu.roll(f_all, sh, 0), 0.0)
        sh *= 2
    lane = lax.broadcasted_iota(jnp.int32, (1, gt.shape[1]), 1)
    sub = lax.broadcasted_iota(jnp.int32, (gt.shape[1], 1), 0)
    col_of = lambda x, idx: jnp.sum(jnp.where(lane == idx, x, 0.0), axis=1, keepdims=True)
    row_of = lambda xt, idx: jnp.sum(jnp.where(sub == idx, xt, 0.0), axis=0, keepdims=True)
    f_col, ig_col = col_of(f_all, heads + h), col_of(ig_all, h)
    f_row, ig_row = row_of(f_all.T, heads + h), row_of(ig_all.T, h)

    m = m_ref[0]
    colidx = lax.broadcasted_iota(jnp.int32, (1, t), 1)
    dlog = jnp.where(row >= colidx, f_col - f_row + ig_row, -jnp.inf)
    inter = f_col + m
    mt = jnp.maximum(inter, jnp.max(dlog, axis=1, keepdims=True))
    w = jnp.exp(dlog - mt)
    a = jnp.exp(inter - mt)
    qb, kb, vb = q.astype(BF16), k.astype(BF16), v.astype(BF16)
    s = _dot_nt(qb, kb) * w
    cmat = c_ref[0]
    num = a * jnp.dot(qb, cmat.astype(BF16), preferred_element_type=F32) \
        + jnp.dot(s.astype(BF16), vb, preferred_element_type=F32)
    nvec = n_ref[0]
    den = a * jnp.sum(q * nvec, axis=1, keepdims=True) + jnp.sum(s, axis=1, keepdims=True)
    hloc = num / jnp.maximum(jnp.abs(den), jnp.exp(-mt))
    m_new = mt[t - 1:t, :]
    f_tot = f_col[t - 1:t, :]
    decay = jnp.exp(f_tot + m - m_new)
    ws = jnp.exp(f_tot - f_col + ig_col - m_new)
    kws = k * ws
    c_ref[0] = decay * cmat + lax.dot_general(kws.astype(BF16), vb, (((0,), (0,)), ((), ())),
                                              preferred_element_type=F32)
    n_ref[0] = decay * nvec + jnp.sum(kws, axis=0, keepdims=True)
    m_ref[0] = m_new

    o = hloc[:t_in, :] * _sigmoid(op_ref[...])
    mu = jnp.mean(o, axis=1, keepdims=True)
    var = jnp.mean(jnp.square(o - mu), axis=1, keepdims=True)
    hn = (o - mu) * lax.rsqrt(var + HEAD_NORM_EPS)
    hn_ref[...] = hn * ng_ref[...] + sk_ref[...] * xc[:t_in, :]


def _mlstm_pallas(proj2d, gates2d, bsz, seq, t_in, t, conv_prev, c0, n0, m0, layer, b_gate, conv_w, conv_b,
                  w_q, w_k, w_v, norm_g, skip):
    heads, dh, _ = w_q.shape
    e = heads * dh
    kw = conv_w.shape[0]
    gl = gates2d.shape[1]
    nl = c0.shape[0]
    assert seq % t_in == 0 and t_in <= t and t_in % SUBLANES == 0 and kw - 1 <= min(CONV_HALO, t_in)
    nc = seq // t_in
    rows = lambda blk: pl.BlockSpec((t_in, dh), lambda h, b, c: (b * nc + c, blk(h)))
    per_head_vec = lambda r: pl.BlockSpec((r, dh), lambda h, b, c: (0, h))
    wspec = pl.BlockSpec((1, dh, dh), lambda h, b, c: (h, 0, 0))
    st = lambda shp, off: pl.BlockSpec((1,) + shp, lambda h, b, c: (off + b * heads + h, 0, 0))
    cpspec = lambda off: pl.BlockSpec((1, kw - 1, dh), lambda h, b, c: (off + b, 0, h))
    sbase = layer * bsz * heads
    hn, cs, c_new, n_new, m_new = pl.pallas_call(
        functools.partial(_mlstm_body, t_in=t_in, t=t, heads=heads),
        grid=(heads, bsz, nc),
        in_specs=[rows(lambda h: h), rows(lambda h: 2 * heads + h),
                  pl.BlockSpec((t_in, gl), lambda h, b, c: (b * nc + c, 0)),
                  pl.BlockSpec((1, gl), lambda h, b, c: (0, 0)),
                  per_head_vec(kw), per_head_vec(1), wspec, wspec, wspec, per_head_vec(1), per_head_vec(1),
                  cpspec(layer * bsz), st((dh, dh), sbase), st((1, dh), sbase), st((1, 1), sbase)],
        out_specs=[rows(lambda h: h), cpspec(0), st((dh, dh), 0), st((1, dh), 0), st((1, 1), 0)],
        out_shape=[jax.ShapeDtypeStruct((bsz * seq, e), F32), jax.ShapeDtypeStruct((bsz, kw - 1, e), F32),
                   jax.ShapeDtypeStruct((bsz * heads, dh, dh), F32), jax.ShapeDtypeStruct((bsz * heads, 1, dh), F32),
                   jax.ShapeDtypeStruct((bsz * heads, 1, 1), F32)],
        scratch_shapes=[pltpu.VMEM((CONV_HALO + t, dh), F32)],
        compiler_params=_cparams("arbitrary", "arbitrary", "arbitrary"),
        name="mlstm",
    )(proj2d, proj2d, gates2d, jnp.pad(b_gate, (0, gl - b_gate.shape[0])).reshape(1, gl),
      conv_w, conv_b.reshape(1, e), w_q.astype(BF16), w_k.astype(BF16), w_v.astype(BF16),
      norm_g.reshape(1, e), skip.reshape(1, e), conv_prev.reshape(nl * bsz, kw - 1, e),
      c0.reshape(nl * bsz * heads, dh, dh), n0.reshape(nl * bsz * heads, 1, dh), m0.reshape(nl * bsz * heads, 1, 1))
    return (hn, cs, c_new.reshape(bsz, heads, dh, dh), n_new.reshape(bsz, heads, dh), m_new.reshape(bsz, heads))


def _s5_disc_body(lr_ref, li_ref, ldt_ref, br_ref, bi_ref, ar_ref, ai_ref, bbr_ref, bbi_ref):
    lr = jnp.minimum(lr_ref[...], -1e-4)
    li = li_ref[...]
    dt = jnp.exp(ldt_ref[...])
    mag = jnp.exp(dt * lr)
    a_re, a_im = mag * jnp.cos(dt * li), mag * jnp.sin(dt * li)
    den = lr * lr + li * li
    xr, xi = a_re - 1.0, a_im
    cr = (xr * lr + xi * li) / den
    ci = (xi * lr - xr * li) / den
    ar_ref[...] = a_re
    ai_ref[...] = a_im
    b_r, b_i = br_ref[...], bi_ref[...]
    bbr_ref[...] = cr * b_r - ci * b_i
    bbi_ref[...] = cr * b_i + ci * b_r


def _s5_discretise_pallas(lam_re, lam_im, log_dt, b_re, b_im):
    g, p, c = b_re.shape
    vm = pl.BlockSpec(memory_space=pltpu.VMEM)
    a_re, a_im, bb_re, bb_im = pl.pallas_call(
        _s5_disc_body,
        in_specs=[vm] * 5,
        out_specs=[vm] * 4,
        out_shape=[jax.ShapeDtypeStruct((g, 1, p), F32)] * 2 + [jax.ShapeDtypeStruct((g, c, p), F32)] * 2,
        name="s5_discretise",
    )(lam_re.reshape(g, 1, p), lam_im.reshape(g, 1, p), log_dt.reshape(g, 1, 1),
      jnp.swapaxes(b_re, 1, 2), jnp.swapaxes(b_im, 1, 2))
    return a_re.reshape(g, p), a_im.reshape(g, p), bb_re, bb_im


S5_STRIP_GROUPS = 8


def _gelu_tanh(x):
    return x * (0.5 * (1.0 + jnp.tanh(math.sqrt(2.0 / math.pi) * (x + 0.044715 * (x * x * x)))))


S5_SEGS = SUBLANES
S5_SEG_CHUNK = 512
S5_DOT_PIECES = 4
S5_STRIPS_PER_STEP = 1
S5_UNROLL = 64


def _cmul(ar, ai, br, bi):
    return ar * br - ai * bi, ar * bi + ai * br


def _s5_seg_body(*refs, chained, nsp):
    u_refs, rest = refs[:nsp], refs[nsp:]
    (wbr_ref, wbi_ref, wcr_ref, wci_ref, ar_ref, ai_ref, d_ref, h0r_ref, h0i_ref, g_ref, hr_ref, hi_ref,
     xr_scr, xi_scr, pwr_scr, pwi_scr, apr_scr, api_scr, gp_scr) = rest
    for ss in range(nsp):
        _s5_strip(pl.program_id(2) * nsp + ss, ss, u_refs[ss], wbr_ref, wbi_ref, wcr_ref, wci_ref, ar_ref, ai_ref,
                  d_ref, h0r_ref, h0i_ref, g_ref, hr_ref, hi_ref, xr_scr.at[ss], xi_scr.at[ss],
                  pwr_scr, pwi_scr, apr_scr, api_scr, gp_scr.at[ss], chained=chained)


def _s5_strip(s, ss, u_ref, wbr_ref, wbi_ref, wcr_ref, wci_ref, ar_ref, ai_ref, d_ref, h0r_ref, h0i_ref,
              g_ref, hr_ref, hi_ref, xr_scr, xi_scr, pwr_scr, pwi_scr, apr_scr, api_scr, gp_scr, *, chained):
    nseg = S5_SEGS
    lu, lst = u_ref.shape[1], xr_scr.shape[1]
    ucols, scols = slice(ss * lu, (ss + 1) * lu), slice(ss * lst, (ss + 1) * lst)
    rows = u_ref.shape[0]
    jn = rows // nseg
    ar, ai = ar_ref[s], ai_ref[s]

    @pl.when((pl.program_id(0) == 0) & (pl.program_id(1) == 0))
    def _():
        pr, pi = ar, ai
        for j in range(jn):
            pwr_scr[s, j:j + 1, :] = pr
            pwi_scr[s, j:j + 1, :] = pi
            if j + 1 < jn:
                pr, pi = _cmul(pr, pi, ar, ai)
        qr, qi = jnp.ones_like(ar), jnp.zeros_like(ai)
        for k in range(nseg + 1):
            apr_scr[s, k:k + 1, :] = qr
            api_scr[s, k:k + 1, :] = qi
            qr, qi = _cmul(qr, qi, pr, pi)

    if chained:
        @pl.when(pl.program_id(1) == 0)
        def _():
            hr_ref[s] = h0r_ref[s]
            hi_ref[s] = h0i_ref[s]

    up = jnp.concatenate([u_ref[pl.ds(j, nseg, stride=jn), :] for j in range(jn)], axis=0)
    ub = up.astype(BF16)
    npc = S5_DOT_PIECES if rows >= S5_DOT_PIECES * LANES else 1
    pieces = [slice(r, r + rows // npc) for r in range(0, rows, rows // npc)]
    for pc in pieces:
        xr_scr[pc, :] = jnp.dot(ub[pc], wbr_ref[s], preferred_element_type=F32)
        xi_scr[pc, :] = jnp.dot(ub[pc], wbi_ref[s], preferred_element_type=F32)
    blk_of = lambda j: pl.ds(pl.multiple_of(j * nseg, nseg), nseg)

    def scan_step(j, h):
        tr, ti = _cmul(ar, ai, h[0], h[1])
        hr, hi = tr + xr_scr[blk_of(j), :], ti + xi_scr[blk_of(j), :]
        xr_scr[blk_of(j), :] = hr
        xi_scr[blk_of(j), :] = hi
        return hr, hi

    zero = jnp.zeros((nseg, ar.shape[1]), F32)
    hr, hi = lax.fori_loop(0, jn, scan_step, (zero, zero), unroll=min(jn, S5_UNROLL))
    anr, ani = pwr_scr[s, jn - 1:jn, :], pwi_scr[s, jn - 1:jn, :]
    if chained:
        sub = lax.broadcasted_iota(jnp.int32, (nseg, 1), 0)
        er, ei, pr, pi = hr, hi, anr, ani
        sh = 1
        while sh < nseg:
            keep = sub >= sh
            tr, ti = _cmul(pr, pi, jnp.where(keep, pltpu.roll(er, sh, 0), 0.0),
                           jnp.where(keep, pltpu.roll(ei, sh, 0), 0.0))
            er, ei = er + tr, ei + ti
            pr, pi = _cmul(pr, pi, pr, pi)
            sh *= 2
        hin_r, hin_i = hr_ref[s], hi_ref[s]
        tr, ti = _cmul(apr_scr[s, 0:nseg, :], api_scr[s, 0:nseg, :], hin_r, hin_i)
        cr = jnp.where(sub >= 1, pltpu.roll(er, 1, 0), 0.0) + tr
        ci = jnp.where(sub >= 1, pltpu.roll(ei, 1, 0), 0.0) + ti
        tr, ti = _cmul(apr_scr[s, 1:nseg + 1, :], api_scr[s, 1:nseg + 1, :], hin_r, hin_i)
        hr_ref[s] = (er + tr)[nseg - 1:nseg, :]
        hi_ref[s] = (ei + ti)[nseg - 1:nseg, :]
    else:
        cr, ci = h0r_ref[:, scols], h0i_ref[:, scols]
        tr, ti = _cmul(anr, ani, cr, ci)
        hr_ref[:, scols] = hr + tr
        hi_ref[:, scols] = hi + ti

    def fix_step(j, carry):
        tr, ti = _cmul(pwr_scr[s, pl.ds(j, 1), :], pwi_scr[s, pl.ds(j, 1), :], cr, ci)
        xr_scr[blk_of(j), :] = xr_scr[blk_of(j), :] + tr
        xi_scr[blk_of(j), :] = xi_scr[blk_of(j), :] + ti
        return carry

    lax.fori_loop(0, jn, fix_step, 0, unroll=min(jn, S5_UNROLL))
    y = jnp.concatenate([jnp.dot(xr_scr[pc, :].astype(BF16), wcr_ref[s], preferred_element_type=F32)
                         - jnp.dot(xi_scr[pc, :].astype(BF16), wci_ref[s], preferred_element_type=F32)
                         for pc in pieces], axis=0)
    g = _gelu_tanh(y + d_ref[:, ucols] * up)
    for j in range(jn):
        gp_scr[pl.ds(j, nseg, stride=jn), :] = g[j * nseg:(j + 1) * nseg, :]
    g_ref[:, ucols] = gp_scr[...]


def _s5_seg_pallas(proj2d, bsz, seq, chunk, h0_re, h0_im, a_re, a_im, bb_re, bb_im, c_re, c_im, d_skip):
    g, cch, p = bb_re.shape
    e = g * cch
    sg = S5_STRIP_GROUPS
    ns = g // sg
    lu, ls = sg * cch, sg * p
    assert g % sg == 0 and lu == LANES
    chained = seq % chunk == 0
    if chained:
        rows, n0, nc = chunk, bsz, seq // chunk
    else:
        assert bsz % S5_SEGS == 0
        rows, n0, nc = S5_SEGS * seq, bsz // S5_SEGS, 1
    jn = rows // S5_SEGS
    eye = jnp.eye(sg, dtype=F32)
    wb = lambda bb: jnp.einsum('sgcp,gh->sgchp', bb.reshape(ns, sg, cch, p), eye).reshape(ns, lu, ls).astype(BF16)
    wc = lambda cc: jnp.einsum('sgcp,gh->sgphc', cc.reshape(ns, sg, cch, p), eye).reshape(ns, ls, lu).astype(BF16)
    const3 = lambda shp: pl.BlockSpec(shp, lambda b, c, s: (0, 0, 0))
    nsp = S5_STRIPS_PER_STEP
    assert ns % nsp == 0
    if chained:
        hspec = pl.BlockSpec((ns, 1, ls), lambda b, c, s: (b, 0, 0))
        hshape = (bsz * ns, 1, ls)
    else:
        hspec = pl.BlockSpec((S5_SEGS, nsp * ls), lambda b, c, s: (b, s))
        hshape = (bsz, ns * ls)
    uspec = lambda ss: pl.BlockSpec((rows, lu), lambda b, c, s: (b * nc + c, s * nsp + ss))
    gout, hr, hi = pl.pallas_call(
        functools.partial(_s5_seg_body, chained=chained, nsp=nsp),
        grid=(n0, nc, ns // nsp),
        in_specs=[uspec(ss) for ss in range(nsp)]
        + [const3((ns, lu, ls)), const3((ns, lu, ls)), const3((ns, ls, lu)), const3((ns, ls, lu)),
           const3((ns, 1, ls)), const3((ns, 1, ls)),
           pl.BlockSpec((1, nsp * lu), lambda b, c, s: (0, s)), hspec, hspec],
        out_specs=[pl.BlockSpec((rows, nsp * lu), lambda b, c, s: (b * nc + c, s)), hspec, hspec],
        out_shape=[jax.ShapeDtypeStruct((bsz * seq, e), F32),
                   jax.ShapeDtypeStruct(hshape, F32), jax.ShapeDtypeStruct(hshape, F32)],
        scratch_shapes=[pltpu.VMEM((nsp, rows, ls), F32), pltpu.VMEM((nsp, rows, ls), F32),
                        pltpu.VMEM((ns, jn, ls), F32), pltpu.VMEM((ns, jn, ls), F32),
                        pltpu.VMEM((ns, 2 * S5_SEGS, ls), F32), pltpu.VMEM((ns, 2 * S5_SEGS, ls), F32),
                        pltpu.VMEM((nsp, rows, lu), F32)],
        compiler_params=_cparams("arbitrary", "arbitrary", "arbitrary"),
        name="s5_scan",
    )(*([proj2d] * nsp), wb(bb_re), wb(bb_im), wc(c_re), wc(c_im),
      a_re.reshape(ns, 1, ls), a_im.reshape(ns, 1, ls), d_skip.reshape(1, e),
      h0_re.reshape(hshape), h0_im.reshape(hshape))
    return gout, hr.reshape(bsz, g, p), hi.reshape(bsz, g, p)


def _glu_out_body(*refs, final):
    if final:
        g_ref, wg_ref, z_ref, x_ref, wo_ref, fg_ref, y_ref = refs
    else:
        g_ref, wg_ref, z_ref, x_ref, wo_ref, y_ref = refs
    g = g_ref[...]
    a = g * _sigmoid(jnp.dot(g.astype(BF16), wg_ref[...], preferred_element_type=F32))
    z = z_ref[...]
    act = (a * (z * _sigmoid(z))).astype(BF16)
    y = x_ref[...] + jnp.dot(act, wo_ref[...], preferred_element_type=F32)
    if final:
        ms = jnp.mean(y * y, axis=-1, keepdims=True)
        y = y * lax.rsqrt(ms + RMS_EPS) * fg_ref[...]
    y_ref[...] = y


def _glu_out(g2d, wg_bf16, z_src, x2d, wo_bf16, tm, final_g=None):
    m, e = g2d.shape
    d = x2d.shape[1]
    tm = min(tm, m)
    assert m % tm == 0 and wg_bf16.shape == (e, e) and wo_bf16.shape == (e, d)
    z_arr, z_blk = z_src
    once = pl.Buffered(1)
    in_specs = [pl.BlockSpec((tm, e), lambda i: (i, 0)),
                pl.BlockSpec((e, e), lambda i: (0, 0), pipeline_mode=once),
                pl.BlockSpec((tm, e), lambda i: (i, z_blk)),
                pl.BlockSpec((tm, d), lambda i: (i, 0)),
                pl.BlockSpec((e, d), lambda i: (0, 0), pipeline_mode=once)]
    args = [g2d, wg_bf16, z_arr, x2d, wo_bf16]
    if final_g is not None:
        in_specs.append(pl.BlockSpec((1, d), lambda i: (0, 0)))
        args.append(final_g.reshape(1, d))
    return pl.pallas_call(
        functools.partial(_glu_out_body, final=final_g is not None),
        grid=(m // tm,),
        in_specs=in_specs,
        out_specs=pl.BlockSpec((tm, d), lambda i: (i, 0)),
        out_shape=jax.ShapeDtypeStruct((m, d), F32),
        compiler_params=_cparams("arbitrary"),
        name="glu_out",
    )(*args)


POOL_WINDOWS = (2, 4, 8, 16)
POOL_HALO = 2 * SUBLANES
POOL_CHUNK = 256


def _pool_body(*refs, windows, start, final):
    if final:
        u_ref, pre_ref, w_ref, sc_ref, z_ref, x_ref, wo_ref, fg_ref, y_ref, st_ref, ext, mix_scr = refs
    else:
        u_ref, pre_ref, w_ref, sc_ref, z_ref, x_ref, wo_ref, y_ref, st_ref, ext, mix_scr = refs
    c = pl.program_id(1)
    t = u_ref.shape[0]
    dp = max(windows) - 1
    dg = u_ref.shape[1] // len(windows)

    @pl.when(c == 0)
    def _():
        ext[POOL_HALO - dp:POOL_HALO, :] = pre_ref[0]

    @pl.when(c > 0)
    def _():
        ext[POOL_HALO - dp:POOL_HALO, :] = ext[POOL_HALO + t - dp:POOL_HALO + t, :]

    ext[POOL_HALO:POOL_HALO + t, :] = u_ref[...]
    st_ref[0] = ext[POOL_HALO + t - dp:POOL_HALO + t, :]
    pos = start + c * t + lax.broadcasted_iota(jnp.int32, (t, 1), 0)
    for g, w in enumerate(windows):
        lo = g * dg
        cur = ext[POOL_HALO:POOL_HALO + t, lo:lo + dg]
        tot = cur
        for j in range(1, w):
            tot = tot + ext[POOL_HALO - j:POOL_HALO - j + t, lo:lo + dg]
        cnt = jnp.minimum(pos + 1, w).astype(F32)
        mix = (tot / cnt - cur).astype(BF16)
        mix_scr[:, lo:lo + dg] = jnp.dot(mix, w_ref[g], preferred_element_type=F32) * sc_ref[:, lo:lo + dg]
    z = z_ref[...]
    act = (mix_scr[...] * (z * _sigmoid(z))).astype(BF16)
    y = x_ref[...] + jnp.dot(act, wo_ref[...], preferred_element_type=F32)
    if final:
        ms = jnp.mean(y * y, axis=-1, keepdims=True)
        y = y * lax.rsqrt(ms + RMS_EPS) * fg_ref[...]
    y_ref[...] = y


def _pool_pallas(proj2d, x2d, bsz, seq, chunk, prefix, start, w_grp, scale, windows, w_out_bf16, final_g=None):
    nw, dg, _ = w_grp.shape
    e = nw * dg
    d = x2d.shape[1]
    dp = max(windows) - 1
    assert seq % chunk == 0 and prefix.shape == (bsz, dp, e) and dp < POOL_HALO and nw == len(windows)
    nc = seq // chunk
    rowblk = lambda width, col: pl.BlockSpec((chunk, width), lambda b, c: (b * nc + c, col))
    in_specs = [rowblk(e, 0),
                pl.BlockSpec((1, dp, e), lambda b, c: (b, 0, 0)),
                pl.BlockSpec((nw, dg, dg), lambda b, c: (0, 0, 0)),
                pl.BlockSpec((1, e), lambda b, c: (0, 0)),
                rowblk(e, 1), rowblk(d, 0),
                pl.BlockSpec((e, d), lambda b, c: (0, 0))]
    args = [proj2d, prefix, w_grp.astype(BF16), scale.reshape(1, e), proj2d, x2d, w_out_bf16]
    if final_g is not None:
        in_specs.append(pl.BlockSpec((1, d), lambda b, c: (0, 0)))
        args.append(final_g.reshape(1, d))
    return pl.pallas_call(
        functools.partial(_pool_body, windows=windows, start=start, final=final_g is not None),
        grid=(bsz, nc),
        in_specs=in_specs,
        out_specs=[rowblk(d, 0), pl.BlockSpec((1, dp, e), lambda b, c: (b, 0, 0))],
        out_shape=[jax.ShapeDtypeStruct((bsz * seq, d), F32), jax.ShapeDtypeStruct((bsz, dp, e), F32)],
        scratch_shapes=[pltpu.VMEM((POOL_HALO + chunk, e), F32), pltpu.VMEM((chunk, e), F32)],
        compiler_params=_cparams("arbitrary", "arbitrary"),
        name="pool",
    )(*args)


DIL_PATTERNS = ((128, 1), (512, 4), (2048, 16))
DIL_BLOCK = 128
DIL_TQ = 2048
DIL_UNROLL = 8
DIL_MERGE_ROWS = 256


def _dil_prompt_body(*refs, dils, dk, nvp):
    q0_ref, q1_ref, q2_ref, k0_ref, k1_ref, k2_ref = refs[:6]
    v_refs = refs[6:6 + nvp]
    o_ref = refs[6 + nvp]
    per_pat = nvp + 2
    scr = refs[7 + nvp:]
    pat_scr = [scr[g * per_pat:(g + 1) * per_pat] for g in range(len(dils))]
    h, i = pl.program_id(1), pl.program_id(2)
    tq = o_ref.shape[0]
    qb = DIL_BLOCK
    nsub = tq // qb
    lane = lax.broadcasted_iota(jnp.int32, (1, LANES), 1)
    mine = (lane // dk) == (h % (LANES // dk))
    rowi = lax.broadcasted_iota(jnp.int32, (qb, 1), 0)
    coli = lax.broadcasted_iota(jnp.int32, (1, qb), 1)
    scale = dk ** -0.5
    for g, d in enumerate(dils):
        q_ref, k_ref = (q0_ref, q1_ref, q2_ref)[g], (k0_ref, k1_ref, k2_ref)[g]
        ld = d.bit_length() - 1

        acc, m_scr, l_scr = pat_scr[g][:nvp], pat_scr[g][nvp], pat_scr[g][nvp + 1]

        def body(idx, carry, q_ref=q_ref, k_ref=k_ref, d=d, ld=ld, acc=acc, m_scr=m_scr, l_scr=l_scr):
            r, bl = idx & (d - 1), idx >> ld
            lstart = r + (d * qb) * bl
            gstart = i * tq + lstart
            has_prev = gstart >= d * qb
            pstart = jnp.where(has_prev, gstart - d * qb, gstart)
            rows = lambda s: pl.ds(s, qb, stride=d) if d > 1 else pl.ds(s, qb)
            qm = (jnp.where(mine, q_ref[rows(lstart), :], 0.0) * scale).astype(BF16)
            sc = _dot_nt(qm, k_ref[rows(gstart), :].astype(BF16))
            sp = _dot_nt(qm, k_ref[rows(pstart), :].astype(BF16))
            sc = jnp.where(coli <= rowi, sc, -jnp.inf)
            sp = jnp.where((coli >= rowi) & has_prev, sp, -jnp.inf)
            mb = jnp.max(jnp.maximum(sc, sp), axis=1, keepdims=True)
            pc, pp = jnp.exp(sc - mb), jnp.exp(sp - mb)
            lb = jnp.sum(pc + pp, axis=1, keepdims=True)
            vrows = lambda s: jnp.concatenate([v[rows(s), :] for v in v_refs], axis=1).astype(BF16)
            nb = (jnp.dot(pc.astype(BF16), vrows(gstart), preferred_element_type=F32)
                  + jnp.dot(pp.astype(BF16), vrows(pstart), preferred_element_type=F32))
            for p, a in enumerate(acc):
                a[rows(lstart), :] = nb[:, p * LANES:(p + 1) * LANES]
            m_scr[rows(lstart), :] = jnp.broadcast_to(mb, (qb, LANES))
            l_scr[rows(lstart), :] = jnp.broadcast_to(lb, (qb, LANES))
            return carry

        lax.fori_loop(0, nsub, body, 0, unroll=DIL_UNROLL)

    def merge(c, carry):
        rs = pl.ds(pl.multiple_of(c * DIL_MERGE_ROWS, DIL_MERGE_ROWS), DIL_MERGE_ROWS)
        ms = [ps[nvp][rs, :] for ps in pat_scr]
        mmax = functools.reduce(jnp.maximum, ms)
        es = [jnp.exp(m - mmax) for m in ms]
        den = sum(ps[nvp + 1][rs, :] * e for ps, e in zip(pat_scr, es))
        for p in range(nvp):
            o_ref[rs, p * LANES:(p + 1) * LANES] = sum(ps[p][rs, :] * e for ps, e in zip(pat_scr, es)) / den
        return carry

    lax.fori_loop(0, tq // DIL_MERGE_ROWS, merge, 0)


def _dil_prompt_pallas(proj2d, bsz, seq, heads, dk, dv, col_q, col_k, col_v):
    dils = tuple(d for _, d in DIL_PATTERNS)
    assert all(w == d * DIL_BLOCK for w, d in DIL_PATTERNS)
    tq = min(DIL_TQ, seq)
    assert seq % tq == 0 and tq % (DIL_BLOCK * max(dils)) == 0 and LANES % dk == 0 and dv % LANES == 0
    nq = seq // tq
    hpb = LANES // dk
    nqk = heads * dk
    nvp = dv // LANES
    qspec = lambda g: pl.BlockSpec((tq, LANES), lambda b, h, i: (b * nq + i, (col_q + g * nqk) // LANES + h // hpb))
    kspec = lambda g: pl.BlockSpec((seq, LANES), lambda b, h, i: (b, (col_k + g * nqk) // LANES + h // hpb))
    vspec = lambda p: pl.BlockSpec((seq, LANES), lambda b, h, i: (b, col_v // LANES + h * nvp + p))
    return pl.pallas_call(
        functools.partial(_dil_prompt_body, dils=dils, dk=dk, nvp=nvp),
        grid=(bsz, heads, nq),
        in_specs=[qspec(0), qspec(1), qspec(2), kspec(0), kspec(1), kspec(2)] + [vspec(p) for p in range(nvp)],
        out_specs=pl.BlockSpec((tq, dv), lambda b, h, i: (b * nq + i, h)),
        out_shape=jax.ShapeDtypeStruct((bsz * seq, heads * dv), F32),
        scratch_shapes=[pltpu.VMEM((tq, LANES), F32)] * ((2 + nvp) * len(dils)),
        compiler_params=_cparams("arbitrary", "arbitrary", "arbitrary"),
        name="dilated_prompt",
    )(*([proj2d] * (6 + nvp)))


DIL_S_ROWS = 512


def _dil_decode_body(q_ref, kn_ref, vn_ref, k1_ref, k2_ref, k3_ref, v_ref, k3x_ref, vx_ref,
                     o_ref, k1o_ref, k2o_ref, k3o_ref, vo_ref, m_scr, l_scr, acc_scr, *, dils, dk):
    c, nch = pl.program_id(1), pl.num_programs(1)
    ls = q_ref.shape[2]
    d2, d3 = dils[1], dils[2]
    rc = v_ref.shape[1] * v_ref.shape[2]
    scale = dk ** -0.5
    lanes = lambda x: jnp.broadcast_to(x, x.shape[:-1] + (LANES,))
    scores = lambda ks, q: jnp.sum(ks * q, axis=-1, keepdims=True)

    @pl.when(c == 0)
    def _():
        m_scr[...] = jnp.full(m_scr.shape, -jnp.inf, F32)
        l_scr[...] = jnp.zeros(l_scr.shape, F32)
        acc_scr[...] = jnp.zeros(acc_scr.shape, F32)

    for i in range(ls):
        q = q_ref[0, 2, i] * scale
        s = scores(k3_ref[0, :, i], q)
        m_old = m_scr[i][:, 0:1]
        m_new = jnp.maximum(m_old, jnp.max(s, axis=0))
        alpha = jnp.exp(m_old - m_new)
        e = jnp.exp(s - m_new)
        m_scr[i] = lanes(m_new)
        l_scr[i] = lanes(l_scr[i][:, 0:1] * alpha + jnp.sum(e, axis=0))
        acc_scr[i] = acc_scr[i] * alpha + jnp.sum(e * v_ref[0, :, i], axis=0)

    hk, hv = k3_ref.shape[3:], v_ref.shape[3:]
    k3o_ref[0, 0:rc - ls] = k3_ref[0].reshape((rc,) + hk)[ls:rc]
    vo_ref[0, 0:rc - ls] = v_ref[0].reshape((rc,) + hv)[ls:rc]

    @pl.when(c < nch - 1)
    def _():
        k3o_ref[0, rc - ls:rc] = k3x_ref[0, 0]
        vo_ref[0, rc - ls:rc] = vx_ref[0, 0]

    @pl.when(c == nch - 1)
    def _():
        k3o_ref[0, rc - ls:rc] = kn_ref[0, 2]
        vo_ref[0, rc - ls:rc] = vn_ref[0]
        l1, l2 = k1_ref.shape[1], k2_ref.shape[1] * d2
        k1o_ref[0, 0:l1 - ls] = k1_ref[0, ls:l1]
        k1o_ref[0, l1 - ls:l1] = kn_ref[0, 0]
        k2o_ref[0, 0:l2 - ls] = k2_ref[0].reshape((l2,) + hk)[ls:l2]
        k2o_ref[0, l2 - ls:l2] = kn_ref[0, 1]
        na = v_ref.shape[1]
        v2 = lambda r: jnp.stack([v_ref[0, na - l2 // d3:na, r + d2 * t] for t in range(d3 // d2)], axis=1) \
            .reshape((l2 // d2,) + hv)
        v1 = v_ref[0, na - l1 // d3:na].reshape((l1,) + hv)
        for i in range(ls):
            parts = []
            q = q_ref[0, 0, i] * scale
            ks = jnp.concatenate([k1_ref[0, i:l1], kn_ref[0, 0, 0:i + 1]], axis=0)
            vs = jnp.concatenate([v1[i:l1], vn_ref[0, 0:i + 1]], axis=0)
            parts.append((scores(ks, q), vs))
            q = q_ref[0, 1, i] * scale
            news = list(range(i % d2, i + 1, d2))
            ks = jnp.concatenate([k2_ref[0, i // d2:, i % d2]] + [kn_ref[0, 1, t:t + 1] for t in news], axis=0)
            vs = jnp.concatenate([v2(i % d2)[i // d2:]] + [vn_ref[0, t:t + 1] for t in news], axis=0)
            parts.append((scores(ks, q), vs))
            q = q_ref[0, 2, i] * scale
            s_new = scores(kn_ref[0, 2, i:i + 1], q)[0]
            m_old = m_scr[i][:, 0:1]
            m3 = jnp.maximum(m_old, s_new)
            alpha, e_new = jnp.exp(m_old - m3), jnp.exp(s_new - m3)
            den3 = l_scr[i][:, 0:1] * alpha + e_new
            num3 = acc_scr[i] * alpha + e_new * vn_ref[0, i]
            stats = []
            for s, vs in parts:
                m = jnp.max(s, axis=0)
                e = jnp.exp(s - m)
                stats.append((jnp.sum(e * vs, axis=0), m, jnp.sum(e, axis=0)))
            stats.append((num3, m3, den3))
            mmax = functools.reduce(jnp.maximum, [m for _, m, _ in stats])
            num = sum(nu * jnp.exp(m - mmax) for nu, m, _ in stats)
            den = sum(de * jnp.exp(m - mmax) for _, m, de in stats)
            o_ref[0, i] = num / den


def _dil_decode_pallas(proj2d, bsz, ls, k_caches, v_cache, layer, col_q, col_k, col_v):
    heads, dk = k_caches[0].shape[3], k_caches[0].shape[4]
    dv = v_cache.shape[4]
    nl = v_cache.shape[0]
    wins, dils = tuple(w for w, _ in DIL_PATTERNS), tuple(d for _, d in DIL_PATTERNS)
    l1, l2, l3 = (kc.shape[2] for kc in k_caches)
    lv = v_cache.shape[2]
    d1, d2, d3 = dils
    rc = DIL_S_ROWS
    assert (l1, l2, l3) == wins and lv == l3 and d1 == 1 and d3 % d2 == 0 and ls == SUBLANES and ls <= d3
    assert lv % rc == 0 and rc % d3 == 0 and l2 <= rc and l1 <= rc and l1 % d3 == 0 and l2 % d3 == 0
    nch = lv // rc
    npat = len(dils)
    nqk = heads * dk
    p3 = proj2d.reshape(bsz, ls, -1)
    pick = lambda col: jnp.swapaxes(p3[:, :, col:col + npat * nqk].reshape(bsz, ls, npat, heads, dk), 1, 2)
    q5, kn5 = pick(col_q), pick(col_k)
    vn4 = p3[:, :, col_v:col_v + heads * dv].reshape(bsz, ls, heads, dv)
    base = layer * bsz
    k1 = k_caches[0].reshape(nl * bsz, l1, heads, dk)
    k2 = k_caches[1].reshape(nl * bsz, l2 // d2, d2, heads, dk)
    k3 = k_caches[2].reshape(nl * bsz, l3 // d3, d3, heads, dk)
    v6 = v_cache.reshape(nl * bsz, lv // d3, d3, heads, dv)
    k3x = k_caches[2].reshape(nl * bsz, l3 // ls, ls, heads, dk)
    vx = v_cache.reshape(nl * bsz, lv // ls, ls, heads, dv)
    nxt = lambda b, c: (base + b, jnp.minimum((c + 1) * (rc // ls), lv // ls - 1), 0, 0, 0)
    bc4 = lambda shp: pl.BlockSpec((1,) + shp, lambda b, c: (b, 0, 0, 0))
    outs = pl.pallas_call(
        functools.partial(_dil_decode_body, dils=dils, dk=dk),
        grid=(bsz, nch),
        in_specs=[pl.BlockSpec((1, npat, ls, heads, dk), lambda b, c: (b, 0, 0, 0, 0)),
                  pl.BlockSpec((1, npat, ls, heads, dk), lambda b, c: (b, 0, 0, 0, 0)),
                  bc4((ls, heads, dv)),
                  pl.BlockSpec((1, l1, heads, dk), lambda b, c: (base + b, 0, 0, 0)),
                  pl.BlockSpec((1, l2 // d2, d2, heads, dk), lambda b, c: (base + b, 0, 0, 0, 0)),
                  pl.BlockSpec((1, rc // d3, d3, heads, dk), lambda b, c: (base + b, c, 0, 0, 0)),
                  pl.BlockSpec((1, rc // d3, d3, heads, dv), lambda b, c: (base + b, c, 0, 0, 0)),
                  pl.BlockSpec((1, 1, ls, heads, dk), nxt),
                  pl.BlockSpec((1, 1, ls, heads, dv), nxt)],
        out_specs=[bc4((ls, heads, dv)), bc4((l1, heads, dk)), bc4((l2, heads, dk)),
                   pl.BlockSpec((1, rc, heads, dk), lambda b, c: (b, c, 0, 0)),
                   pl.BlockSpec((1, rc, heads, dv), lambda b, c: (b, c, 0, 0))],
        out_shape=[jax.ShapeDtypeStruct((bsz, ls, heads, dv), F32), jax.ShapeDtypeStruct((bsz, l1, heads, dk), F32),
                   jax.ShapeDtypeStruct((bsz, l2, heads, dk), F32), jax.ShapeDtypeStruct((bsz, l3, heads, dk), F32),
                   jax.ShapeDtypeStruct((bsz, lv, heads, dv), F32)],
        scratch_shapes=[pltpu.VMEM((ls, heads, LANES), F32), pltpu.VMEM((ls, heads, LANES), F32),
                        pltpu.VMEM((ls, heads, dv), F32)],
        compiler_params=_cparams("arbitrary", "arbitrary"),
        name="dilated_decode",
    )(q5, kn5, vn4, k1, k2, k3, v6, k3x, vx)
    o, nk1, nk2, nk3, nv = outs
    return o.reshape(bsz * ls, heads * dv), (nk1, nk2, nk3), nv


def _mlstm_chunks(seq):
    if seq % MLSTM_CHUNK == 0:
        return MLSTM_CHUNK, MLSTM_CHUNK
    return seq, max(MLSTM_MIN_CHUNK, seq)


def _chunk_of(seq, chunk):
    return chunk if seq % chunk == 0 else seq


def kernel(x_prompt, x_sample, state_mlstm_c, state_mlstm_n, state_mlstm_m, state_mlstm_conv, state_s5_re, state_s5_im, cache_dil_k1, cache_dil_k2, cache_dil_k3, cache_dil_v, state_pool, norm_g, final_norm_g, a_w_in, a_b_gate, a_conv_w, a_conv_b, a_w_q, a_w_k, a_w_v, a_norm_g, a_skip, a_w_out, b_w_in, b_lam_re, b_lam_im, b_log_dt, b_B_re, b_B_im, b_C_re, b_C_im, b_d, b_w_glu, b_w_out, c_w_in, c_w_out, d_w_in, d_w_grp, d_scale, d_w_out):
    bp, lp, dm = x_prompt.shape
    bs, ls, _ = x_sample.shape
    depth = norm_g.shape[0]
    H, Dh = a_w_q.shape[1], a_w_q.shape[2]
    E = H * Dh
    heads, dk = cache_dil_k1.shape[3], cache_dil_k1.shape[4]
    dv = cache_dil_v.shape[4]
    nqk = heads * dk
    npat = len(DIL_PATTERNS)
    names = ('mlstm_c', 'mlstm_n', 'mlstm_m', 'mlstm_conv', 's5_re', 's5_im', 'k1', 'k2', 'k3', 'v', 'pool')
    new_p = {nm: [] for nm in names}
    new_s = {nm: [] for nm in names}
    yp = x_prompt.reshape(bp * lp, dm)
    ys = x_sample.reshape(bs * ls, dm)
    zeros = lambda *shape: jnp.zeros(shape, F32)

    def proj_both(layer, w):
        wb = w.astype(BF16)
        return (_norm_matmul(yp, norm_g[layer], wb, PROJ_TM, PROJ_TN),
                _norm_matmul(ys, norm_g[layer], wb, PROJ_TM, PROJ_TN))

    for layer in range(depth):
        kind, j = layer % N_MIXERS, layer // N_MIXERS
        fg = final_norm_g if layer == depth - 1 else None
        if kind == 0:
            pp, ps = proj_both(layer, a_w_in[j][:, :3 * E])
            gp, gs = proj_both(layer, jnp.pad(a_w_in[j][:, 3 * E:], ((0, 0), (0, LANES - 2 * H))))
            w = (a_b_gate[j], a_conv_w[j], a_conv_b[j], a_w_q[j], a_w_k[j], a_w_v[j], a_norm_g[j], a_skip[j])
            ap, *sp = _mlstm_pallas(pp, gp, bp, lp, *_mlstm_chunks(lp), zeros(1, bp, a_conv_w.shape[1] - 1, E),
                                    zeros(1, bp, H, Dh, Dh), zeros(1, bp, H, Dh), zeros(1, bp, H), 0, *w)
            as_, *ss = _mlstm_pallas(ps, gs, bs, ls, *_mlstm_chunks(ls), state_mlstm_conv, state_mlstm_c,
                                     state_mlstm_n, state_mlstm_m, j, *w)
            keys = ('mlstm_conv', 'mlstm_c', 'mlstm_n', 'mlstm_m')
            zblk, w_out = 1, a_w_out[j]
        elif kind == 1:
            pp, ps = proj_both(layer, b_w_in[j])
            a_re, a_im, bb_re, bb_im = _s5_discretise_pallas(b_lam_re[j], b_lam_im[j], b_log_dt[j], b_B_re[j], b_B_im[j])
            w = (a_re, a_im, bb_re, bb_im, b_C_re[j], b_C_im[j], b_d[j])
            zs = zeros(bp, *state_s5_re.shape[2:])
            gp_, *sp = _s5_seg_pallas(pp, bp, lp, S5_SEG_CHUNK, zs, zs, *w)
            gs_, *ss = _s5_seg_pallas(ps, bs, ls, S5_SEG_CHUNK, state_s5_re[j], state_s5_im[j], *w)
            wglu, wo = b_w_glu[j].astype(BF16), b_w_out[j].astype(BF16)
            yp = _glu_out(gp_, wglu, (pp, 1), yp, wo, GLU_TM, fg)
            ys = _glu_out(gs_, wglu, (ps, 1), ys, wo, GLU_TM, fg)
            keys = ('s5_re', 's5_im')
            w_out = None
        elif kind == 2:
            wc = c_w_in[j]
            wc = jnp.concatenate([wc[:, 2 * npat * nqk + E:], wc[:, 2 * npat * nqk:2 * npat * nqk + E],
                                  wc[:, :2 * npat * nqk]], axis=1)
            col_v, col_q, col_k = E, 2 * E, 2 * E + npat * nqk
            pp, ps = proj_both(layer, wc)
            ap = _dil_prompt_pallas(pp, bp, lp, heads, dk, dv, col_q, col_k, col_v)
            pp3 = pp.reshape(bp, lp, -1)
            sp = [pp3[:, lp - min(win, lp):, col_k + g * nqk:col_k + (g + 1) * nqk].reshape(bp, -1, heads, dk)
                  for g, (win, _) in enumerate(DIL_PATTERNS)]
            sp.append(pp3[:, lp - min(DIL_PATTERNS[-1][0], lp):, col_v:col_v + E].reshape(bp, -1, heads, dv))
            as_, kq, vq = _dil_decode_pallas(ps, bs, ls, (cache_dil_k1, cache_dil_k2, cache_dil_k3), cache_dil_v, j,
                                             col_q, col_k, col_v)
            ss = (*kq, vq)
            keys = ('k1', 'k2', 'k3', 'v')
            zblk, w_out = 0, c_w_out[j]
        else:
            pp, ps = proj_both(layer, d_w_in[j])
            dp = max(POOL_WINDOWS) - 1
            wo = d_w_out[j].astype(BF16)
            yp, *sp = _pool_pallas(pp, yp, bp, lp, _chunk_of(lp, POOL_CHUNK), zeros(bp, dp, E), 0, d_w_grp[j],
                                   d_scale[j], POOL_WINDOWS, wo, fg)
            ys, *ss = _pool_pallas(ps, ys, bs, ls, _chunk_of(ls, POOL_CHUNK), state_pool[j], PAST_LEN, d_w_grp[j],
                                   d_scale[j], POOL_WINDOWS, wo, fg)
            keys = ('pool',)
            w_out = None
        for nm, a, b in zip(keys, sp, ss):
            new_p[nm].append(a)
            new_s[nm].append(b)
        if w_out is not None:
            wo = w_out.astype(BF16)
            yp = _gated_out(yp, (ap, 0), (pp, zblk), wo, OUT_TM, fg)
            ys = _gated_out(ys, (as_, 0), (ps, zblk), wo, OUT_TM, fg)
    out = [yp.reshape(bp, lp, dm), ys.reshape(bs, ls, dm)]
    for nm in names:
        out.append(jnp.stack(new_p[nm]))
        out.append(jnp.stack(new_s[nm]))
    return tuple(out)
```

```python
import functools
import math

import jax
import jax.numpy as jnp
from jax import lax
from jax.experimental import pallas as pl
from jax.experimental.pallas import tpu as pltpu

F32 = jnp.float32
BF16 = jnp.bfloat16
RMS_EPS = 1e-6
HEAD_NORM_EPS = 1e-6
LANES = 128
SUBLANES = 8
VMEM_LIMIT = 48 * 1024 * 1024
PAST_LEN = 8192
N_MIXERS = 4
PROJ_TM, PROJ_TN = 2048, 1024
OUT_TM = 512
GLU_TM = 512


def _cparams(*sem):
    return pltpu.CompilerParams(dimension_semantics=sem, vmem_limit_bytes=VMEM_LIMIT)


def _sigmoid(x):
    return 1.0 / (1.0 + jnp.exp(-x))


def _dot_nt(a, b):
    return lax.dot_general(a, b, (((1,), (1,)), ((), ())), preferred_element_type=F32)


def _norm_matmul_body(x_ref, g_ref, w_ref, o_ref, h_scr):
    @pl.when(pl.program_id(1) == 0)
    def _():
        xf = x_ref[...]
        ms = jnp.mean(xf * xf, axis=-1, keepdims=True)
        h_scr[...] = (xf * lax.rsqrt(ms + RMS_EPS) * g_ref[...]).astype(BF16)

    o_ref[...] = jnp.dot(h_scr[...], w_ref[...], preferred_element_type=F32)


def _norm_matmul(x2d, g, w_bf16, tm, tn):
    m, k = x2d.shape
    n = w_bf16.shape[1]
    tm, tn = min(tm, m), min(tn, n)
    assert m % tm == 0 and n % tn == 0
    return pl.pallas_call(
        _norm_matmul_body,
        grid=(m // tm, n // tn),
        in_specs=[pl.BlockSpec((tm, k), lambda i, j: (i, 0)),
                  pl.BlockSpec((1, k), lambda i, j: (0, 0)),
                  pl.BlockSpec((k, tn), lambda i, j: (0, j))],
        out_specs=pl.BlockSpec((tm, tn), lambda i, j: (i, j)),
        out_shape=jax.ShapeDtypeStruct((m, n), F32),
        scratch_shapes=[pltpu.VMEM((tm, k), BF16)],
        compiler_params=_cparams("arbitrary", "arbitrary"),
        name="norm_matmul",
    )(x2d, g.reshape(1, k), w_bf16)


def _gated_out_body(*refs, final):
    if final:
        x_ref, a_ref, z_ref, w_ref, fg_ref, o_ref = refs
    else:
        x_ref, a_ref, z_ref, w_ref, o_ref = refs
    z = z_ref[...]
    act = (a_ref[...] * (z * _sigmoid(z))).astype(BF16)
    y = x_ref[...] + jnp.dot(act, w_ref[...], preferred_element_type=F32)
    if final:
        ms = jnp.mean(y * y, axis=-1, keepdims=True)
        y = y * lax.rsqrt(ms + RMS_EPS) * fg_ref[...]
    o_ref[...] = y


def _gated_out(x2d, a_src, z_src, w_bf16, tm, final_g=None):
    m, d = x2d.shape
    e = w_bf16.shape[0]
    tm = min(tm, m)
    assert m % tm == 0
    (a_arr, a_blk), (z_arr, z_blk) = a_src, z_src
    in_specs = [pl.BlockSpec((tm, d), lambda i: (i, 0)),
                pl.BlockSpec((tm, e), lambda i: (i, a_blk)),
                pl.BlockSpec((tm, e), lambda i: (i, z_blk)),
                pl.BlockSpec((e, d), lambda i: (0, 0))]
    args = [x2d, a_arr, z_arr, w_bf16]
    if final_g is not None:
        in_specs.append(pl.BlockSpec((1, d), lambda i: (0, 0)))
        args.append(final_g.reshape(1, d))
    return pl.pallas_call(
        functools.partial(_gated_out_body, final=final_g is not None),
        grid=(m // tm,),
        in_specs=in_specs,
        out_specs=pl.BlockSpec((tm, d), lambda i: (i, 0)),
        out_shape=jax.ShapeDtypeStruct((m, d), F32),
        compiler_params=_cparams("arbitrary"),
        name="gated_out",
    )(*args)


MLSTM_CHUNK = 256
MLSTM_MIN_CHUNK = 128
CONV_HALO = SUBLANES


def _log_sigmoid(x):
    return jnp.minimum(x, 0.0) - jnp.log1p(jnp.exp(-jnp.abs(x)))


def _mlstm_body(xm_ref, op_ref, gt_ref, bg_ref, cw_ref, cb_ref, wq_ref, wk_ref, wv_ref, ng_ref, sk_ref,
                cp_ref, c0_ref, n0_ref, m0_ref,
                hn_ref, cs_ref, c_ref, n_ref, m_ref, ext, *, t_in, t, heads):
    h, c = pl.program_id(0), pl.program_id(2)
    kw = cw_ref.shape[0]
    dh = xm_ref.shape[1]
    lo = CONV_HALO - (kw - 1)

    @pl.when(c == 0)
    def _():
        c_ref[...] = c0_ref[...]
        n_ref[...] = n0_ref[...]
        m_ref[...] = m0_ref[...]
        ext[lo:CONV_HALO, :] = cp_ref[0]
        if t_in < t:
            ext[CONV_HALO + t_in:CONV_HALO + t, :] = jnp.zeros((t - t_in, dh), F32)

    @pl.when(c > 0)
    def _():
        ext[lo:CONV_HALO, :] = ext[lo + t_in:CONV_HALO + t_in, :]

    ext[CONV_HALO:CONV_HALO + t_in, :] = xm_ref[...]
    cs_ref[0] = ext[lo + t_in:CONV_HALO + t_in, :]

    xm = ext[CONV_HALO:CONV_HALO + t, :]
    xconv = cb_ref[...] + ext[lo:lo + t, :] * cw_ref[0:1, :]
    for i in range(1, kw):
        xconv = xconv + ext[lo + i:lo + i + t, :] * cw_ref[i:i + 1, :]
    xc = xconv * _sigmoid(xconv)
    xcb = xc.astype(BF16)
    q = jnp.dot(xcb, wq_ref[0], preferred_element_type=F32)
    k = jnp.dot(xcb, wk_ref[0], preferred_element_type=F32) * (dh ** -0.5)
    v = jnp.dot(xm.astype(BF16), wv_ref[0], preferred_element_type=F32)

    gt = gt_ref[...] + bg_ref[...]
    row = lax.broadcasted_iota(jnp.int32, (t, 1), 0)
    if t_in < t:
        gt = jnp.concatenate([gt, jnp.zeros((t - t_in, gt.shape[1]), F32)], axis=0)
        valid = row < t_in
        ig_all = jnp.where(valid, gt, -jnp.inf)
        f_all = jnp.where(valid, _log_sigmoid(gt), 0.0)
    else:
        ig_all = gt
        f_all = _log_sigmoid(gt)
    sh = 1
    while sh < t:
        f_all = f_all + jnp.where(row >= sh, pltpu.roll(f_all, sh, 0), 0.0)
        sh *= 2
    lane = lax.broadcasted_iota(jnp.int32, (1, gt.shape[1]), 1)
    sub = lax.broadcasted_iota(jnp.int32, (gt.shape[1], 1), 0)
    col_of = lambda x, idx: jnp.sum(jnp.where(lane == idx, x, 0.0), axis=1, keepdims=True)
    row_of = lambda xt, idx: jnp.sum(jnp.where(sub == idx, xt, 0.0), axis=0, keepdims=True)
    f_col, ig_col = col_of(f_all, heads + h), col_of(ig_all, h)
    f_row, ig_row = row_of(f_all.T, heads + h), row_of(ig_all.T, h)

    m = m_ref[0]
    colidx = lax.broadcasted_iota(jnp.int32, (1, t), 1)
    dlog = jnp.where(row >= colidx, f_col - f_row + ig_row, -jnp.inf)
    inter = f_col + m
    mt = jnp.maximum(inter, jnp.max(dlog, axis=1, keepdims=True))
    w = jnp.exp(dlog - mt)
    a = jnp.exp(inter - mt)
    qb, kb, vb = q.astype(BF16), k.astype(BF16), v.astype(BF16)
    s = _dot_nt(qb, kb) * w
    cmat = c_ref[0]
    num = a * jnp.dot(qb, cmat.astype(BF16), preferred_element_type=F32) \
        + jnp.dot(s.astype(BF16), vb, preferred_element_type=F32)
    nvec = n_ref[0]
    den = a * jnp.sum(q * nvec, axis=1, keepdims=True) + jnp.sum(s, axis=1, keepdims=True)
    hloc = num / jnp.maximum(jnp.abs(den), jnp.exp(-mt))
    m_new = mt[t - 1:t, :]
    f_tot = f_col[t - 1:t, :]
    decay = jnp.exp(f_tot + m - m_new)
    ws = jnp.exp(f_tot - f_col + ig_col - m_new)
    kws = k * ws
    c_ref[0] = decay * cmat + lax.dot_general(kws.astype(BF16), vb, (((0,), (0,)), ((), ())),
                                              preferred_element_type=F32)
    n_ref[0] = decay * nvec + jnp.sum(kws, axis=0, keepdims=True)
    m_ref[0] = m_new

    o = hloc[:t_in, :] * _sigmoid(op_ref[...])
    mu = jnp.mean(o, axis=1, keepdims=True)
    var = jnp.mean(jnp.square(o - mu), axis=1, keepdims=True)
    hn = (o - mu) * lax.rsqrt(var + HEAD_NORM_EPS)
    hn_ref[...] = hn * ng_ref[...] + sk_ref[...] * xc[:t_in, :]


def _mlstm_pallas(proj2d, gates2d, bsz, seq, t_in, t, conv_prev, c0, n0, m0, layer, b_gate, conv_w, conv_b,
                  w_q, w_k, w_v, norm_g, skip):
    heads, dh, _ = w_q.shape
    e = heads * dh
    kw = conv_w.shape[0]
    gl = gates2d.shape[1]
    nl = c0.shape[0]
    assert seq % t_in == 0 and t_in <= t and t_in % SUBLANES == 0 and kw - 1 <= min(CONV_HALO, t_in)
    nc = seq // t_in
    rows = lambda blk: pl.BlockSpec((t_in, dh), lambda h, b, c: (b * nc + c, blk(h)))
    per_head_vec = lambda r: pl.BlockSpec((r, dh), lambda h, b, c: (0, h))
    wspec = pl.BlockSpec((1, dh, dh), lambda h, b, c: (h, 0, 0))
    st = lambda shp, off: pl.BlockSpec((1,) + shp, lambda h, b, c: (off + b * heads + h, 0, 0))
    cpspec = lambda off: pl.BlockSpec((1, kw - 1, dh), lambda h, b, c: (off + b, 0, h))
    sbase = layer * bsz * heads
    hn, cs, c_new, n_new, m_new = pl.pallas_call(
        functools.partial(_mlstm_body, t_in=t_in, t=t, heads=heads),
        grid=(heads, bsz, nc),
        in_specs=[rows(lambda h: h), rows(lambda h: 2 * heads + h),
                  pl.BlockSpec((t_in, gl), lambda h, b, c: (b * nc + c, 0)),
                  pl.BlockSpec((1, gl), lambda h, b, c: (0, 0)),
                  per_head_vec(kw), per_head_vec(1), wspec, wspec, wspec, per_head_vec(1), per_head_vec(1),
                  cpspec(layer * bsz), st((dh, dh), sbase), st((1, dh), sbase), st((1, 1), sbase)],
        out_specs=[rows(lambda h: h), cpspec(0), st((dh, dh), 0), st((1, dh), 0), st((1, 1), 0)],
        out_shape=[jax.ShapeDtypeStruct((bsz * seq, e), F32), jax.ShapeDtypeStruct((bsz, kw - 1, e), F32),
                   jax.ShapeDtypeStruct((bsz * heads, dh, dh), F32), jax.ShapeDtypeStruct((bsz * heads, 1, dh), F32),
                   jax.ShapeDtypeStruct((bsz * heads, 1, 1), F32)],
        scratch_shapes=[pltpu.VMEM((CONV_HALO + t, dh), F32)],
        compiler_params=_cparams("arbitrary", "arbitrary", "arbitrary"),
        name="mlstm",
    )(proj2d, proj2d, gates2d, jnp.pad(b_gate, (0, gl - b_gate.shape[0])).reshape(1, gl),
      conv_w, conv_b.reshape(1, e), w_q.astype(BF16), w_k.astype(BF16), w_v.astype(BF16),
      norm_g.reshape(1, e), skip.reshape(1, e), conv_prev.reshape(nl * bsz, kw - 1, e),
      c0.reshape(nl * bsz * heads, dh, dh), n0.reshape(nl * bsz * heads, 1, dh), m0.reshape(nl * bsz * heads, 1, 1))
    return (hn, cs, c_new.reshape(bsz, heads, dh, dh), n_new.reshape(bsz, heads, dh), m_new.reshape(bsz, heads))


def _s5_disc_body(lr_ref, li_ref, ldt_ref, br_ref, bi_ref, ar_ref, ai_ref, bbr_ref, bbi_ref):
    lr = jnp.minimum(lr_ref[...], -1e-4)
    li = li_ref[...]
    dt = jnp.exp(ldt_ref[...])
    mag = jnp.exp(dt * lr)
    a_re, a_im = mag * jnp.cos(dt * li), mag * jnp.sin(dt * li)
    den = lr * lr + li * li
    xr, xi = a_re - 1.0, a_im
    cr = (xr * lr + xi * li) / den
    ci = (xi * lr - xr * li) / den
    ar_ref[...] = a_re
    ai_ref[...] = a_im
    b_r, b_i = br_ref[...], bi_ref[...]
    bbr_ref[...] = cr * b_r - ci * b_i
    bbi_ref[...] = cr * b_i + ci * b_r


def _s5_discretise_pallas(lam_re, lam_im, log_dt, b_re, b_im):
    g, p, c = b_re.shape
    vm = pl.BlockSpec(memory_space=pltpu.VMEM)
    a_re, a_im, bb_re, bb_im = pl.pallas_call(
        _s5_disc_body,
        in_specs=[vm] * 5,
        out_specs=[vm] * 4,
        out_shape=[jax.ShapeDtypeStruct((g, 1, p), F32)] * 2 + [jax.ShapeDtypeStruct((g, c, p), F32)] * 2,
        name="s5_discretise",
    )(lam_re.reshape(g, 1, p), lam_im.reshape(g, 1, p), log_dt.reshape(g, 1, 1),
      jnp.swapaxes(b_re, 1, 2), jnp.swapaxes(b_im, 1, 2))
    return a_re.reshape(g, p), a_im.reshape(g, p), bb_re, bb_im


S5_STRIP_GROUPS = 8


def _gelu_tanh(x):
    return x * (0.5 * (1.0 + jnp.tanh(math.sqrt(2.0 / math.pi) * (x + 0.044715 * (x * x * x)))))


S5_SEGS = SUBLANES
S5_SEG_CHUNK = 512
S5_DOT_PIECES = 4
S5_STRIPS_PER_STEP = 1
S5_UNROLL = 64


def _cmul(ar, ai, br, bi):
    return ar * br - ai * bi, ar * bi + ai * br


def _s5_seg_body(*refs, chained, nsp):
    u_refs, rest = refs[:nsp], refs[nsp:]
    (wbr_ref, wbi_ref, wcr_ref, wci_ref, ar_ref, ai_ref, d_ref, h0r_ref, h0i_ref, g_ref, hr_ref, hi_ref,
     xr_scr, xi_scr, pwr_scr, pwi_scr, apr_scr, api_scr, gp_scr) = rest
    for ss in range(nsp):
        _s5_strip(pl.program_id(2) * nsp + ss, ss, u_refs[ss], wbr_ref, wbi_ref, wcr_ref, wci_ref, ar_ref, ai_ref,
                  d_ref, h0r_ref, h0i_ref, g_ref, hr_ref, hi_ref, xr_scr.at[ss], xi_scr.at[ss],
                  pwr_scr, pwi_scr, apr_scr, api_scr, gp_scr.at[ss], chained=chained)


def _s5_strip(s, ss, u_ref, wbr_ref, wbi_ref, wcr_ref, wci_ref, ar_ref, ai_ref, d_ref, h0r_ref, h0i_ref,
              g_ref, hr_ref, hi_ref, xr_scr, xi_scr, pwr_scr, pwi_scr, apr_scr, api_scr, gp_scr, *, chained):
    nseg = S5_SEGS
    lu, lst = u_ref.shape[1], xr_scr.shape[1]
    ucols, scols = slice(ss * lu, (ss + 1) * lu), slice(ss * lst, (ss + 1) * lst)
    rows = u_ref.shape[0]
    jn = rows // nseg
    ar, ai = ar_ref[s], ai_ref[s]

    @pl.when((pl.program_id(0) == 0) & (pl.program_id(1) == 0))
    def _():
        pr, pi = ar, ai
        for j in range(jn):
            pwr_scr[s, j:j + 1, :] = pr
            pwi_scr[s, j:j + 1, :] = pi
            if j + 1 < jn:
                pr, pi = _cmul(pr, pi, ar, ai)
        qr, qi = jnp.ones_like(ar), jnp.zeros_like(ai)
        for k in range(nseg + 1):
            apr_scr[s, k:k + 1, :] = qr
            api_scr[s, k:k + 1, :] = qi
            qr, qi = _cmul(qr, qi, pr, pi)

    if chained:
        @pl.when(pl.program_id(1) == 0)
        def _():
            hr_ref[s] = h0r_ref[s]
            hi_ref[s] = h0i_ref[s]

    up = jnp.concatenate([u_ref[pl.ds(j, nseg, stride=jn), :] for j in range(jn)], axis=0)
    ub = up.astype(BF16)
    npc = S5_DOT_PIECES if rows >= S5_DOT_PIECES * LANES else 1
    pieces = [slice(r, r + rows // npc) for r in range(0, rows, rows // npc)]
    for pc in pieces:
        xr_scr[pc, :] = jnp.dot(ub[pc], wbr_ref[s], preferred_element_type=F32)
        xi_scr[pc, :] = jnp.dot(ub[pc], wbi_ref[s], preferred_element_type=F32)
    blk_of = lambda j: pl.ds(pl.multiple_of(j * nseg, nseg), nseg)

    def scan_step(j, h):
        tr, ti = _cmul(ar, ai, h[0], h[1])
        hr, hi = tr + xr_scr[blk_of(j), :], ti + xi_scr[blk_of(j), :]
        xr_scr[blk_of(j), :] = hr
        xi_scr[blk_of(j), :] = hi
        return hr, hi

    zero = jnp.zeros((nseg, ar.shape[1]), F32)
    hr, hi = lax.fori_loop(0, jn, scan_step, (zero, zero), unroll=min(jn, S5_UNROLL))
    anr, ani = pwr_scr[s, jn - 1:jn, :], pwi_scr[s, jn - 1:jn, :]
    if chained:
        sub = lax.broadcasted_iota(jnp.int32, (nseg, 1), 0)
        er, ei, pr, pi = hr, hi, anr, ani
        sh = 1
        while sh < nseg:
            keep = sub >= sh
            tr, ti = _cmul(pr, pi, jnp.where(keep, pltpu.roll(er, sh, 0), 0.0),
                           jnp.where(keep, pltpu.roll(ei, sh, 0), 0.0))
            er, ei = er + tr, ei + ti
            pr, pi = _cmul(pr, pi, pr, pi)
            sh *= 2
        hin_r, hin_i = hr_ref[s], hi_ref[s]
        tr, ti = _cmul(apr_scr[s, 0:nseg, :], api_scr[s, 0:nseg, :], hin_r, hin_i)
        cr = jnp.where(sub >= 1, pltpu.roll(er, 1, 0), 0.0) + tr
        ci = jnp.where(sub >= 1, pltpu.roll(ei, 1, 0), 0.0) + ti
        tr, ti = _cmul(apr_scr[s, 1:nseg + 1, :], api_scr[s, 1:nseg + 1, :], hin_r, hin_i)
        hr_ref[s] = (er + tr)[nseg - 1:nseg, :]
        hi_ref[s] = (ei + ti)[nseg - 1:nseg, :]
    else:
        cr, ci = h0r_ref[:, scols], h0i_ref[:, scols]
        tr, ti = _cmul(anr, ani, cr, ci)
        hr_ref[:, scols] = hr + tr
        hi_ref[:, scols] = hi + ti

    def fix_step(j, carry):
        tr, ti = _cmul(pwr_scr[s, pl.ds(j, 1), :], pwi_scr[s, pl.ds(j, 1), :], cr, ci)
        xr_scr[blk_of(j), :] = xr_scr[blk_of(j), :] + tr
        xi_scr[blk_of(j), :] = xi_scr[blk_of(j), :] + ti
        return carry

    lax.fori_loop(0, jn, fix_step, 0, unroll=min(jn, S5_UNROLL))
    y = jnp.concatenate([jnp.dot(xr_scr[pc, :].astype(BF16), wcr_ref[s], preferred_element_type=F32)
                         - jnp.dot(xi_scr[pc, :].astype(BF16), wci_ref[s], preferred_element_type=F32)
                         for pc in pieces], axis=0)
    g = _gelu_tanh(y + d_ref[:, ucols] * up)
    for j in range(jn):
        gp_scr[pl.ds(j, nseg, stride=jn), :] = g[j * nseg:(j + 1) * nseg, :]
    g_ref[:, ucols] = gp_scr[...]


def _s5_seg_pallas(proj2d, bsz, seq, chunk, h0_re, h0_im, a_re, a_im, bb_re, bb_im, c_re, c_im, d_skip):
    g, cch, p = bb_re.shape
    e = g * cch
    sg = S5_STRIP_GROUPS
    ns = g // sg
    lu, ls = sg * cch, sg * p
    assert g % sg == 0 and lu == LANES
    chained = seq % chunk == 0
    if chained:
        rows, n0, nc = chunk, bsz, seq // chunk
    else:
        assert bsz % S5_SEGS == 0
        rows, n0, nc = S5_SEGS * seq, bsz // S5_SEGS, 1
    jn = rows // S5_SEGS
    eye = jnp.eye(sg, dtype=F32)
    wb = lambda bb: jnp.einsum('sgcp,gh->sgchp', bb.reshape(ns, sg, cch, p), eye).reshape(ns, lu, ls).astype(BF16)
    wc = lambda cc: jnp.einsum('sgcp,gh->sgphc', cc.reshape(ns, sg, cch, p), eye).reshape(ns, ls, lu).astype(BF16)
    const3 = lambda shp: pl.BlockSpec(shp, lambda b, c, s: (0, 0, 0))
    nsp = S5_STRIPS_PER_STEP
    assert ns % nsp == 0
    if chained:
        hspec = pl.BlockSpec((ns, 1, ls), lambda b, c, s: (b, 0, 0))
        hshape = (bsz * ns, 1, ls)
    else:
        hspec = pl.BlockSpec((S5_SEGS, nsp * ls), lambda b, c, s: (b, s))
        hshape = (bsz, ns * ls)
    uspec = lambda ss: pl.BlockSpec((rows, lu), lambda b, c, s: (b * nc + c, s * nsp + ss))
    gout, hr, hi = pl.pallas_call(
        functools.partial(_s5_seg_body, chained=chained, nsp=nsp),
        grid=(n0, nc, ns // nsp),
        in_specs=[uspec(ss) for ss in range(nsp)]
        + [const3((ns, lu, ls)), const3((ns, lu, ls)), const3((ns, ls, lu)), const3((ns, ls, lu)),
           const3((ns, 1, ls)), const3((ns, 1, ls)),
           pl.BlockSpec((1, nsp * lu), lambda b, c, s: (0, s)), hspec, hspec],
        out_specs=[pl.BlockSpec((rows, nsp * lu), lambda b, c, s: (b * nc + c, s)), hspec, hspec],
        out_shape=[jax.ShapeDtypeStruct((bsz * seq, e), F32),
                   jax.ShapeDtypeStruct(hshape, F32), jax.ShapeDtypeStruct(hshape, F32)],
        scratch_shapes=[pltpu.VMEM((nsp, rows, ls), F32), pltpu.VMEM((nsp, rows, ls), F32),
                        pltpu.VMEM((ns, jn, ls), F32), pltpu.VMEM((ns, jn, ls), F32),
                        pltpu.VMEM((ns, 2 * S5_SEGS, ls), F32), pltpu.VMEM((ns, 2 * S5_SEGS, ls), F32),
                        pltpu.VMEM((nsp, rows, lu), F32)],
        compiler_params=_cparams("arbitrary", "arbitrary", "arbitrary"),
        name="s5_scan",
    )(*([proj2d] * nsp), wb(bb_re), wb(bb_im), wc(c_re), wc(c_im),
      a_re.reshape(ns, 1, ls), a_im.reshape(ns, 1, ls), d_skip.reshape(1, e),
      h0_re.reshape(hshape), h0_im.reshape(hshape))
    return gout, hr.reshape(bsz, g, p), hi.reshape(bsz, g, p)


def _glu_out_body(*refs, final):
    if final:
        g_ref, wg_ref, z_ref, x_ref, wo_ref, fg_ref, y_ref = refs
    else:
        g_ref, wg_ref, z_ref, x_ref, wo_ref, y_ref = refs
    g = g_ref[...]
    a = g * _sigmoid(jnp.dot(g.astype(BF16), wg_ref[...], preferred_element_type=F32))
    z = z_ref[...]
    act = (a * (z * _sigmoid(z))).astype(BF16)
    y = x_ref[...] + jnp.dot(act, wo_ref[...], preferred_element_type=F32)
    if final:
        ms = jnp.mean(y * y, axis=-1, keepdims=True)
        y = y * lax.rsqrt(ms + RMS_EPS) * fg_ref[...]
    y_ref[...] = y


def _glu_out(g2d, wg_bf16, z_src, x2d, wo_bf16, tm, final_g=None):
    m, e = g2d.shape
    d = x2d.shape[1]
    tm = min(tm, m)
    assert m % tm == 0 and wg_bf16.shape == (e, e) and wo_bf16.shape == (e, d)
    z_arr, z_blk = z_src
    once = pl.Buffered(1)
    in_specs = [pl.BlockSpec((tm, e), lambda i: (i, 0)),
                pl.BlockSpec((e, e), lambda i: (0, 0), pipeline_mode=once),
                pl.BlockSpec((tm, e), lambda i: (i, z_blk)),
                pl.BlockSpec((tm, d), lambda i: (i, 0)),
                pl.BlockSpec((e, d), lambda i: (0, 0), pipeline_mode=once)]
    args = [g2d, wg_bf16, z_arr, x2d, wo_bf16]
    if final_g is not None:
        in_specs.append(pl.BlockSpec((1, d), lambda i: (0, 0)))
        args.append(final_g.reshape(1, d))
    return pl.pallas_call(
        functools.partial(_glu_out_body, final=final_g is not None),
        grid=(m // tm,),
        in_specs=in_specs,
        out_specs=pl.BlockSpec((tm, d), lambda i: (i, 0)),
        out_shape=jax.ShapeDtypeStruct((m, d), F32),
        compiler_params=_cparams("arbitrary"),
        name="glu_out",
    )(*args)


POOL_WINDOWS = (2, 4, 8, 16)
POOL_HALO = 2 * SUBLANES
POOL_CHUNK = 256


def _pool_body(*refs, windows, start, final):
    if final:
        u_ref, pre_ref, w_ref, sc_ref, z_ref, x_ref, wo_ref, fg_ref, y_ref, st_ref, ext, mix_scr = refs
    else:
        u_ref, pre_ref, w_ref, sc_ref, z_ref, x_ref, wo_ref, y_ref, st_ref, ext, mix_scr = refs
    c = pl.program_id(1)
    t = u_ref.shape[0]
    dp = max(windows) - 1
    dg = u_ref.shape[1] // len(windows)

    @pl.when(c == 0)
    def _():
        ext[POOL_HALO - dp:POOL_HALO, :] = pre_ref[0]

    @pl.when(c > 0)
    def _():
        ext[POOL_HALO - dp:POOL_HALO, :] = ext[POOL_HALO + t - dp:POOL_HALO + t, :]

    ext[POOL_HALO:POOL_HALO + t, :] = u_ref[...]
    st_ref[0] = ext[POOL_HALO + t - dp:POOL_HALO + t, :]
    pos = start + c * t + lax.broadcasted_iota(jnp.int32, (t, 1), 0)
    for g, w in enumerate(windows):
        lo = g * dg
        cur = ext[POOL_HALO:POOL_HALO + t, lo:lo + dg]
        tot = cur
        for j in range(1, w):
            tot = tot + ext[POOL_HALO - j:POOL_HALO - j + t, lo:lo + dg]
        cnt = jnp.minimum(pos + 1, w).astype(F32)
        mix = (tot / cnt - cur).astype(BF16)
        mix_scr[:, lo:lo + dg] = jnp.dot(mix, w_ref[g], preferred_element_type=F32) * sc_ref[:, lo:lo + dg]
    z = z_ref[...]
    act = (mix_scr[...] * (z * _sigmoid(z))).astype(BF16)
    y = x_ref[...] + jnp.dot(act, wo_ref[...], preferred_element_type=F32)
    if final:
        ms = jnp.mean(y * y, axis=-1, keepdims=True)
        y = y * lax.rsqrt(ms + RMS_EPS) * fg_ref[...]
    y_ref[...] = y


def _pool_pallas(proj2d, x2d, bsz, seq, chunk, prefix, start, w_grp, scale, windows, w_out_bf16, final_g=None):
    nw, dg, _ = w_grp.shape
    e = nw * dg
    d = x2d.shape[1]
    dp = max(windows) - 1
    assert seq % chunk == 0 and prefix.shape == (bsz, dp, e) and dp < POOL_HALO and nw == len(windows)
    nc = seq // chunk
    rowblk = lambda width, col: pl.BlockSpec((chunk, width), lambda b, c: (b * nc + c, col))
    in_specs = [rowblk(e, 0),
                pl.BlockSpec((1, dp, e), lambda b, c: (b, 0, 0)),
                pl.BlockSpec((nw, dg, dg), lambda b, c: (0, 0, 0)),
                pl.BlockSpec((1, e), lambda b, c: (0, 0)),
                rowblk(e, 1), rowblk(d, 0),
                pl.BlockSpec((e, d), lambda b, c: (0, 0))]
    args = [proj2d, prefix, w_grp.astype(BF16), scale.reshape(1, e), proj2d, x2d, w_out_bf16]
    if final_g is not None:
        in_specs.append(pl.BlockSpec((1, d), lambda b, c: (0, 0)))
        args.append(final_g.reshape(1, d))
    return pl.pallas_call(
        functools.partial(_pool_body, windows=windows, start=start, final=final_g is not None),
        grid=(bsz, nc),
        in_specs=in_specs,
        out_specs=[rowblk(d, 0), pl.BlockSpec((1, dp, e), lambda b, c: (b, 0, 0))],
        out_shape=[jax.ShapeDtypeStruct((bsz * seq, d), F32), jax.ShapeDtypeStruct((bsz, dp, e), F32)],
        scratch_shapes=[pltpu.VMEM((POOL_HALO + chunk, e), F32), pltpu.VMEM((chunk, e), F32)],
        compiler_params=_cparams("arbitrary", "arbitrary"),
        name="pool",
    )(*args)


DIL_PATTERNS = ((128, 1), (512, 4), (2048, 16))
DIL_BLOCK = 128
DIL_TQ = 2048
DIL_UNROLL = 8
DIL_MERGE_ROWS = 256


def _dil_prompt_body(*refs, dils, dk, nvp):
    q0_ref, q1_ref, q2_ref, k0_ref, k1_ref, k2_ref = refs[:6]
    v_refs = refs[6:6 + nvp]
    o_ref = refs[6 + nvp]
    per_pat = nvp + 2
    scr = refs[7 + nvp:]
    pat_scr = [scr[g * per_pat:(g + 1) * per_pat] for g in range(len(dils))]
    h, i = pl.program_id(1), pl.program_id(2)
    tq = o_ref.shape[0]
    qb = DIL_BLOCK
    nsub = tq // qb
    lane = lax.broadcasted_iota(jnp.int32, (1, LANES), 1)
    mine = (lane // dk) == (h % (LANES // dk))
    rowi = lax.broadcasted_iota(jnp.int32, (qb, 1), 0)
    coli = lax.broadcasted_iota(jnp.int32, (1, qb), 1)
    scale = dk ** -0.5
    for g, d in enumerate(dils):
        q_ref, k_ref = (q0_ref, q1_ref, q2_ref)[g], (k0_ref, k1_ref, k2_ref)[g]
        ld = d.bit_length() - 1

        acc, m_scr, l_scr = pat_scr[g][:nvp], pat_scr[g][nvp], pat_scr[g][nvp + 1]

        def body(idx, carry, q_ref=q_ref, k_ref=k_ref, d=d, ld=ld, acc=acc, m_scr=m_scr, l_scr=l_scr):
            r, bl = idx & (d - 1), idx >> ld
            lstart = r + (d * qb) * bl
            gstart = i * tq + lstart
            has_prev = gstart >= d * qb
            pstart = jnp.where(has_prev, gstart - d * qb, gstart)
            rows = lambda s: pl.ds(s, qb, stride=d) if d > 1 else pl.ds(s, qb)
            qm = (jnp.where(mine, q_ref[rows(lstart), :], 0.0) * scale).astype(BF16)
            sc = _dot_nt(qm, k_ref[rows(gstart), :].astype(BF16))
            sp = _dot_nt(qm, k_ref[rows(pstart), :].astype(BF16))
            sc = jnp.where(coli <= rowi, sc, -jnp.inf)
            sp = jnp.where((coli >= rowi) & has_prev, sp, -jnp.inf)
            mb = jnp.max(jnp.maximum(sc, sp), axis=1, keepdims=True)
            pc, pp = jnp.exp(sc - mb), jnp.exp(sp - mb)
            lb = jnp.sum(pc + pp, axis=1, keepdims=True)
            vrows = lambda s: jnp.concatenate([v[rows(s), :] for v in v_refs], axis=1).astype(BF16)
            nb = (jnp.dot(pc.astype(BF16), vrows(gstart), preferred_element_type=F32)
                  + jnp.dot(pp.astype(BF16), vrows(pstart), preferred_element_type=F32))
            for p, a in enumerate(acc):
                a[rows(lstart), :] = nb[:, p * LANES:(p + 1) * LANES]
            m_scr[rows(lstart), :] = jnp.broadcast_to(mb, (qb, LANES))
            l_scr[rows(lstart), :] = jnp.broadcast_to(lb, (qb, LANES))
            return carry

        lax.fori_loop(0, nsub, body, 0, unroll=DIL_UNROLL)

    def merge(c, carry):
        rs = pl.ds(pl.multiple_of(c * DIL_MERGE_ROWS, DIL_MERGE_ROWS), DIL_MERGE_ROWS)
        ms = [ps[nvp][rs, :] for ps in pat_scr]
        mmax = functools.reduce(jnp.maximum, ms)
        es = [jnp.exp(m - mmax) for m in ms]
        den = sum(ps[nvp + 1][rs, :] * e for ps, e in zip(pat_scr, es))
        for p in range(nvp):
            o_ref[rs, p * LANES:(p + 1) * LANES] = sum(ps[p][rs, :] * e for ps, e in zip(pat_scr, es)) / den
        return carry

    lax.fori_loop(0, tq // DIL_MERGE_ROWS, merge, 0)


def _dil_prompt_pallas(proj2d, bsz, seq, heads, dk, dv, col_q, col_k, col_v):
    dils = tuple(d for _, d in DIL_PATTERNS)
    assert all(w == d * DIL_BLOCK for w, d in DIL_PATTERNS)
    tq = min(DIL_TQ, seq)
    assert seq % tq == 0 and tq % (DIL_BLOCK * max(dils)) == 0 and LANES % dk == 0 and dv % LANES == 0
    nq = seq // tq
    hpb = LANES // dk
    nqk = heads * dk
    nvp = dv // LANES
    qspec = lambda g: pl.BlockSpec((tq, LANES), lambda b, h, i: (b * nq + i, (col_q + g * nqk) // LANES + h // hpb))
    kspec = lambda g: pl.BlockSpec((seq, LANES), lambda b, h, i: (b, (col_k + g * nqk) // LANES + h // hpb))
    vspec = lambda p: pl.BlockSpec((seq, LANES), lambda b, h, i: (b, col_v // LANES + h * nvp + p))
    return pl.pallas_call(
        functools.partial(_dil_prompt_body, dils=dils, dk=dk, nvp=nvp),
        grid=(bsz, heads, nq),
        in_specs=[qspec(0), qspec(1), qspec(2), kspec(0), kspec(1), kspec(2)] + [vspec(p) for p in range(nvp)],
        out_specs=pl.BlockSpec((tq, dv), lambda b, h, i: (b * nq + i, h)),
        out_shape=jax.ShapeDtypeStruct((bsz * seq, heads * dv), F32),
        scratch_shapes=[pltpu.VMEM((tq, LANES), F32)] * ((2 + nvp) * len(dils)),
        compiler_params=_cparams("arbitrary", "arbitrary", "arbitrary"),
        name="dilated_prompt",
    )(*([proj2d] * (6 + nvp)))


DIL_S_ROWS = 512


def _dil_decode_body(q_ref, kn_ref, vn_ref, k1_ref, k2_ref, k3_ref, v_ref, k3x_ref, vx_ref, vl_ref, idx_ref, nb_ref,
                     o_ref, k1o_ref, k2o_ref, k3o_ref, vo_ref, m_scr, l_scr, acc_scr, nnum_scr, nm_scr, nl_scr,
                     *, wins, dils, dk, nch):
    c = pl.program_id(1)
    ls = q_ref.shape[2]
    d2, d3 = dils[1], dils[2]
    rc = v_ref.shape[1] * v_ref.shape[2]
    scale = dk ** -0.5
    lanes = lambda x: jnp.broadcast_to(x, x.shape[:-1] + (LANES,))
    scores = lambda ks, q: jnp.sum(ks * q, axis=-1, keepdims=True)

    l1 = k1_ref.shape[1]
    na = vl_ref.shape[1]
    hv = vl_ref.shape[3:]
    nu = d3 // d2
    for t in range(ls // nch):
        i = c * (ls // nch) + t
        r = i & (d2 - 1)
        for g in range(2):
            q = q_ref[0, g, i] * scale
            kc = k1_ref[0] if g == 0 else k2_ref[0, :, r]
            first = i if g == 0 else i // d2
            sc = scores(kc, q) + jnp.where(idx_ref[...] >= first, 0.0, -jnp.inf)
            sn = scores(kn_ref[0, g], q) + nb_ref[g, i]
            m = jnp.maximum(jnp.max(sc, axis=0), jnp.max(sn, axis=0))
            ec, en = jnp.exp(sc - m), jnp.exp(sn - m)
            num = jnp.sum(en * vn_ref[0], axis=0)
            if g == 0:
                num = num + jnp.sum(ec * vl_ref[0, na - l1 // d3:na].reshape((l1,) + hv), axis=0)
            else:
                e4 = ec.reshape((l1 // nu, nu) + ec.shape[1:])
                for u in range(nu):
                    num = num + jnp.sum(e4[:, u] * vl_ref[0, :, r + d2 * u], axis=0)
            nnum_scr[g, i] = num
            nm_scr[g, i] = lanes(m)
            nl_scr[g, i] = lanes(jnp.sum(ec, axis=0) + jnp.sum(en, axis=0))

    @pl.when(c == 0)
    def _():
        m_scr[...] = jnp.full(m_scr.shape, -jnp.inf, F32)
        l_scr[...] = jnp.zeros(l_scr.shape, F32)
        acc_scr[...] = jnp.zeros(acc_scr.shape, F32)

    for i in range(ls):
        q = q_ref[0, 2, i] * scale
        s = scores(k3_ref[0, :, i], q)
        m_old = m_scr[i][:, 0:1]
        m_new = jnp.maximum(m_old, jnp.max(s, axis=0))
        alpha = jnp.exp(m_old - m_new)
        e = jnp.exp(s - m_new)
        m_scr[i] = lanes(m_new)
        l_scr[i] = lanes(l_scr[i][:, 0:1] * alpha + jnp.sum(e, axis=0))
        acc_scr[i] = acc_scr[i] * alpha + jnp.sum(e * v_ref[0, :, i], axis=0)

    hk, hv = k3_ref.shape[3:], v_ref.shape[3:]
    k3o_ref[0, 0:rc - ls] = k3_ref[0].reshape((rc,) + hk)[ls:rc]
    vo_ref[0, 0:rc - ls] = v_ref[0].reshape((rc,) + hv)[ls:rc]

    @pl.when(c < nch - 1)
    def _():
        k3o_ref[0, rc - ls:rc] = k3x_ref[0, 0]
        vo_ref[0, rc - ls:rc] = vx_ref[0, 0]

    @pl.when(c == nch - 1)
    def _():
        k3o_ref[0, rc - ls:rc] = kn_ref[0, 2]
        vo_ref[0, rc - ls:rc] = vn_ref[0]
        l2 = k2_ref.shape[1] * d2
        k1o_ref[0, 0:l1 - ls] = k1_ref[0, ls:l1]
        k1o_ref[0, l1 - ls:l1] = kn_ref[0, 0]
        k2o_ref[0, 0:l2 - ls] = k2_ref[0].reshape((l2,) + hk)[ls:l2]
        k2o_ref[0, l2 - ls:l2] = kn_ref[0, 1]
        for i in range(ls):
            q = q_ref[0, 2, i] * scale
            s_new = scores(kn_ref[0, 2, i:i + 1], q)[0]
            m_old = m_scr[i][:, 0:1]
            m3 = jnp.maximum(m_old, s_new)
            alpha, e_new = jnp.exp(m_old - m3), jnp.exp(s_new - m3)
            den3 = l_scr[i][:, 0:1] * alpha + e_new
            num3 = acc_scr[i] * alpha + e_new * vn_ref[0, i]
            stats = [(nnum_scr[g, i], nm_scr[g, i][:, 0:1], nl_scr[g, i][:, 0:1]) for g in range(2)]
            stats.append((num3, m3, den3))
            mmax = functools.reduce(jnp.maximum, [m for _, m, _ in stats])
            num = sum(nu * jnp.exp(m - mmax) for nu, m, _ in stats)
            den = sum(de * jnp.exp(m - mmax) for _, m, de in stats)
            o_ref[0, i] = num / den


def _dil_decode_pallas(proj2d, bsz, ls, k_caches, v_cache, layer, col_q, col_k, col_v):
    heads, dk = k_caches[0].shape[3], k_caches[0].shape[4]
    dv = v_cache.shape[4]
    nl = v_cache.shape[0]
    wins, dils = tuple(w for w, _ in DIL_PATTERNS), tuple(d for _, d in DIL_PATTERNS)
    l1, l2, l3 = (kc.shape[2] for kc in k_caches)
    lv = v_cache.shape[2]
    d1, d2, d3 = dils
    rc = DIL_S_ROWS
    assert (l1, l2, l3) == wins and lv == l3 and d1 == 1 and d3 % d2 == 0 and ls == SUBLANES and ls <= d3
    assert lv % rc == 0 and rc % d3 == 0 and l2 <= rc and l1 <= rc and l1 % d3 == 0 and l2 % d3 == 0
    nch = lv // rc
    npat = len(dils)
    nqk = heads * dk
    p3 = proj2d.reshape(bsz, ls, -1)
    pick = lambda col: jnp.swapaxes(p3[:, :, col:col + npat * nqk].reshape(bsz, ls, npat, heads, dk), 1, 2)
    q5, kn5 = pick(col_q), pick(col_k)
    vn4 = p3[:, :, col_v:col_v + heads * dv].reshape(bsz, ls, heads, dv)
    base = layer * bsz
    k1 = k_caches[0].reshape(nl * bsz, l1, heads, dk)
    k2 = k_caches[1].reshape(nl * bsz, l2 // d2, d2, heads, dk)
    k3 = k_caches[2].reshape(nl * bsz, l3 // d3, d3, heads, dk)
    v6 = v_cache.reshape(nl * bsz, lv // d3, d3, heads, dv)
    k3x = k_caches[2].reshape(nl * bsz, l3 // ls, ls, heads, dk)
    vx = v_cache.reshape(nl * bsz, lv // ls, ls, heads, dv)
    nxt = lambda b, c: (base + b, jnp.minimum((c + 1) * (rc // ls), lv // ls - 1), 0, 0, 0)
    bc4 = lambda shp: pl.BlockSpec((1,) + shp, lambda b, c: (b, 0, 0, 0))
    assert ls % nch == 0 and l2 // d2 == l1 and l2 == rc
    idx = jnp.arange(l1, dtype=jnp.int32).reshape(l1, 1, 1)
    back = jnp.arange(ls)[:, None] - jnp.arange(ls)[None, :]
    nb = jnp.stack([jnp.where((back >= 0) & (back % d == 0), 0.0, -jnp.inf) for d in (d1, d2)]).astype(F32)
    nb = nb.reshape(2, ls, ls, 1, 1)
    outs = pl.pallas_call(
        functools.partial(_dil_decode_body, wins=wins, dils=dils, dk=dk, nch=nch),
        grid=(bsz, nch),
        in_specs=[pl.BlockSpec((1, npat, ls, heads, dk), lambda b, c: (b, 0, 0, 0, 0)),
                  pl.BlockSpec((1, npat, ls, heads, dk), lambda b, c: (b, 0, 0, 0, 0)),
                  bc4((ls, heads, dv)),
                  pl.BlockSpec((1, l1, heads, dk), lambda b, c: (base + b, 0, 0, 0)),
                  pl.BlockSpec((1, l2 // d2, d2, heads, dk), lambda b, c: (base + b, 0, 0, 0, 0)),
                  pl.BlockSpec((1, rc // d3, d3, heads, dk), lambda b, c: (base + b, c, 0, 0, 0)),
                  pl.BlockSpec((1, rc // d3, d3, heads, dv), lambda b, c: (base + b, c, 0, 0, 0)),
                  pl.BlockSpec((1, 1, ls, heads, dk), nxt),
                  pl.BlockSpec((1, 1, ls, heads, dv), nxt),
                  pl.BlockSpec((1, rc // d3, d3, heads, dv), lambda b, c: (base + b, nch - 1, 0, 0, 0)),
                  pl.BlockSpec((l1, 1, 1), lambda b, c: (0, 0, 0)),
                  pl.BlockSpec((2, ls, ls, 1, 1), lambda b, c: (0, 0, 0, 0, 0))],
        out_specs=[bc4((ls, heads, dv)), bc4((l1, heads, dk)), bc4((l2, heads, dk)),
                   pl.BlockSpec((1, rc, heads, dk), lambda b, c: (b, c, 0, 0)),
                   pl.BlockSpec((1, rc, heads, dv), lambda b, c: (b, c, 0, 0))],
        out_shape=[jax.ShapeDtypeStruct((bsz, ls, heads, dv), F32), jax.ShapeDtypeStruct((bsz, l1, heads, dk), F32),
                   jax.ShapeDtypeStruct((bsz, l2, heads, dk), F32), jax.ShapeDtypeStruct((bsz, l3, heads, dk), F32),
                   jax.ShapeDtypeStruct((bsz, lv, heads, dv), F32)],
        scratch_shapes=[pltpu.VMEM((ls, heads, LANES), F32), pltpu.VMEM((ls, heads, LANES), F32),
                        pltpu.VMEM((ls, heads, dv), F32),
                        pltpu.VMEM((2, ls, heads, dv), F32), pltpu.VMEM((2, ls, heads, LANES), F32),
                        pltpu.VMEM((2, ls, heads, LANES), F32)],
        compiler_params=_cparams("arbitrary", "arbitrary"),
        name="dilated_decode",
    )(q5, kn5, vn4, k1, k2, k3, v6, k3x, vx, v6, idx, nb)
    o, nk1, nk2, nk3, nv = outs
    return o.reshape(bsz * ls, heads * dv), (nk1, nk2, nk3), nv


def _mlstm_chunks(seq):
    if seq % MLSTM_CHUNK == 0:
        return MLSTM_CHUNK, MLSTM_CHUNK
    return seq, max(MLSTM_MIN_CHUNK, seq)


def _chunk_of(seq, chunk):
    return chunk if seq % chunk == 0 else seq


def kernel(x_prompt, x_sample, state_mlstm_c, state_mlstm_n, state_mlstm_m, state_mlstm_conv, state_s5_re, state_s5_im, cache_dil_k1, cache_dil_k2, cache_dil_k3, cache_dil_v, state_pool, norm_g, final_norm_g, a_w_in, a_b_gate, a_conv_w, a_conv_b, a_w_q, a_w_k, a_w_v, a_norm_g, a_skip, a_w_out, b_w_in, b_lam_re, b_lam_im, b_log_dt, b_B_re, b_B_im, b_C_re, b_C_im, b_d, b_w_glu, b_w_out, c_w_in, c_w_out, d_w_in, d_w_grp, d_scale, d_w_out):
    bp, lp, dm = x_prompt.shape
    bs, ls, _ = x_sample.shape
    depth = norm_g.shape[0]
    H, Dh = a_w_q.shape[1], a_w_q.shape[2]
    E = H * Dh
    heads, dk = cache_dil_k1.shape[3], cache_dil_k1.shape[4]
    dv = cache_dil_v.shape[4]
    nqk = heads * dk
    npat = len(DIL_PATTERNS)
    names = ('mlstm_c', 'mlstm_n', 'mlstm_m', 'mlstm_conv', 's5_re', 's5_im', 'k1', 'k2', 'k3', 'v', 'pool')
    new_p = {nm: [] for nm in names}
    new_s = {nm: [] for nm in names}
    yp = x_prompt.reshape(bp * lp, dm)
    ys = x_sample.reshape(bs * ls, dm)
    zeros = lambda *shape: jnp.zeros(shape, F32)

    def proj_both(layer, w):
        wb = w.astype(BF16)
        return (_norm_matmul(yp, norm_g[layer], wb, PROJ_TM, PROJ_TN),
                _norm_matmul(ys, norm_g[layer], wb, PROJ_TM, PROJ_TN))

    for layer in range(depth):
        kind, j = layer % N_MIXERS, layer // N_MIXERS
        fg = final_norm_g if layer == depth - 1 else None
        if kind == 0:
            pp, ps = proj_both(layer, a_w_in[j][:, :3 * E])
            gp, gs = proj_both(layer, jnp.pad(a_w_in[j][:, 3 * E:], ((0, 0), (0, LANES - 2 * H))))
            w = (a_b_gate[j], a_conv_w[j], a_conv_b[j], a_w_q[j], a_w_k[j], a_w_v[j], a_norm_g[j], a_skip[j])
            ap, *sp = _mlstm_pallas(pp, gp, bp, lp, *_mlstm_chunks(lp), zeros(1, bp, a_conv_w.shape[1] - 1, E),
                                    zeros(1, bp, H, Dh, Dh), zeros(1, bp, H, Dh), zeros(1, bp, H), 0, *w)
            as_, *ss = _mlstm_pallas(ps, gs, bs, ls, *_mlstm_chunks(ls), state_mlstm_conv, state_mlstm_c,
                                     state_mlstm_n, state_mlstm_m, j, *w)
            keys = ('mlstm_conv', 'mlstm_c', 'mlstm_n', 'mlstm_m')
            zblk, w_out = 1, a_w_out[j]
        elif kind == 1:
            pp, ps = proj_both(layer, b_w_in[j])
            a_re, a_im, bb_re, bb_im = _s5_discretise_pallas(b_lam_re[j], b_lam_im[j], b_log_dt[j], b_B_re[j], b_B_im[j])
            w = (a_re, a_im, bb_re, bb_im, b_C_re[j], b_C_im[j], b_d[j])
            zs = zeros(bp, *state_s5_re.shape[2:])
            gp_, *sp = _s5_seg_pallas(pp, bp, lp, S5_SEG_CHUNK, zs, zs, *w)
            gs_, *ss = _s5_seg_pallas(ps, bs, ls, S5_SEG_CHUNK, state_s5_re[j], state_s5_im[j], *w)
            wglu, wo = b_w_glu[j].astype(BF16), b_w_out[j].astype(BF16)
            yp = _glu_out(gp_, wglu, (pp, 1), yp, wo, GLU_TM, fg)
            ys = _glu_out(gs_, wglu, (ps, 1), ys, wo, GLU_TM, fg)
            keys = ('s5_re', 's5_im')
            w_out = None
        elif kind == 2:
            wc = c_w_in[j]
            wc = jnp.concatenate([wc[:, 2 * npat * nqk + E:], wc[:, 2 * npat * nqk:2 * npat * nqk + E],
                                  wc[:, :2 * npat * nqk]], axis=1)
            col_v, col_q, col_k = E, 2 * E, 2 * E + npat * nqk
            pp, ps = proj_both(layer, wc)
            ap = _dil_prompt_pallas(pp, bp, lp, heads, dk, dv, col_q, col_k, col_v)
            pp3 = pp.reshape(bp, lp, -1)
            sp = [pp3[:, lp - min(win, lp):, col_k + g * nqk:col_k + (g + 1) * nqk].reshape(bp, -1, heads, dk)
                  for g, (win, _) in enumerate(DIL_PATTERNS)]
            sp.append(pp3[:, lp - min(DIL_PATTERNS[-1][0], lp):, col_v:col_v + E].reshape(bp, -1, heads, dv))
            as_, kq, vq = _dil_decode_pallas(ps, bs, ls, (cache_dil_k1, cache_dil_k2, cache_dil_k3), cache_dil_v, j,
                                             col_q, col_k, col_v)
            ss = (*kq, vq)
            keys = ('k1', 'k2', 'k3', 'v')
            zblk, w_out = 0, c_w_out[j]
        else:
            pp, ps = proj_both(layer, d_w_in[j])
            dp = max(POOL_WINDOWS) - 1
            wo = d_w_out[j].astype(BF16)
            yp, *sp = _pool_pallas(pp, yp, bp, lp, _chunk_of(lp, POOL_CHUNK), zeros(bp, dp, E), 0, d_w_grp[j],
                                   d_scale[j], POOL_WINDOWS, wo, fg)
            ys, *ss = _pool_pallas(ps, ys, bs, ls, _chunk_of(ls, POOL_CHUNK), state_pool[j], PAST_LEN, d_w_grp[j],
                                   d_scale[j], POOL_WINDOWS, wo, fg)
            keys = ('pool',)
            w_out = None
        for nm, a, b in zip(keys, sp, ss):
            new_p[nm].append(a)
            new_s[nm].append(b)
        if w_out is not None:
            wo = w_out.astype(BF16)
            yp = _gated_out(yp, (ap, 0), (pp, zblk), wo, OUT_TM, fg)
            ys = _gated_out(ys, (as_, 0), (ps, zblk), wo, OUT_TM, fg)
    out = [yp.reshape(bp, lp, dm), ys.reshape(bs, ls, dm)]
    for nm in names:
        out.append(jnp.stack(new_p[nm]))
        out.append(jnp.stack(new_s[nm]))
    return tuple(out)
```
